```python
import jax
import jax.numpy as jnp
from jax import lax
import numpy as np

D_MODEL = 1024
BATCH = 8
SEQ = 2048
DEPTH = 4
DEC_BATCH = 128
DEC_SEQ = 1
PAST_LEN = 2048
PAGE_SIZE = 128

HEAD_DIM = 64
H_FOX = 4
H_DSA = 4
H_HG = 8
HG_DK = 64
HG_DV = 64
N_IDX_HEADS = 8
IDX_DIM = 32
FOX_W = H_FOX * HEAD_DIM
DSA_W = H_DSA * HEAD_DIM
HG_KW = H_HG * HG_DK
HG_VW = H_HG * HG_DV
D_MIX = FOX_W + DSA_W + HG_VW
SPLIT_SIZES = (FOX_W, FOX_W, FOX_W, H_FOX, DSA_W, DSA_W, DSA_W, N_IDX_HEADS * IDX_DIM, IDX_DIM, N_IDX_HEADS, HG_KW, HG_KW, HG_VW, HG_VW)
N_IN = 3 * FOX_W + H_FOX + 3 * DSA_W + N_IDX_HEADS * IDX_DIM + IDX_DIM + N_IDX_HEADS + 2 * HG_KW + 2 * HG_VW
Q_BLOCK = 128
HG_CHUNK = 64
TOPK_MAX = 256
ROPE_THETA = 500000.0
N_GROUPS = 4
EXPERTS_PER_GROUP = 4
N_EXPERTS = N_GROUPS * EXPERTS_PER_GROUP
TOP_K_INNER = 2
D_FF_EXPERT = 512
MOE_BLOCK = 128
LN_EPS = 1e-5
DEEPNORM_ALPHA = (2 * DEPTH) ** 0.25
DEEPNORM_BETA = (8 * DEPTH) ** -0.25
FOX_GATE_BIAS = 3.0
F32 = jnp.float32

kernel_name = 'hybrid_fox_dsa_hgrn2_hmoe_step'


def layer_norm(x, g, b):
    xf = x.astype(F32)
    mu = jnp.mean(xf, axis=-1, keepdims=True)
    var = jnp.mean(jnp.square(xf - mu), axis=-1, keepdims=True)
    return ((xf - mu) * lax.rsqrt(var + LN_EPS)).astype(x.dtype) * g + b


def post_norm(x, sub, g, b):
    return layer_norm(DEEPNORM_ALPHA * x + sub, g, b)


def rope_partial(x, pos):
    d = x.shape[-1]
    rot = d // 4
    half = rot // 2
    inv = ROPE_THETA ** (-jnp.arange(half, dtype=F32) * 2.0 / rot)
    ang = pos.astype(F32)[:, None] * inv[None, :]
    cos = jnp.cos(ang)[:, None, :]
    sin = jnp.sin(ang)[:, None, :]
    xf = x.astype(F32)
    x1 = xf[..., :half]
    x2 = xf[..., half:rot]
    out = jnp.concatenate([x1 * cos - x2 * sin, x2 * cos + x1 * sin, xf[..., rot:]], axis=-1)
    return out.astype(x.dtype)


def project(h, w_in_l):
    B, T, _ = h.shape
    cuts = np.cumsum(np.array(SPLIT_SIZES))[:-1].tolist()
    (fq, fk, fv, ff, dq, dk, dv, iq, ik, iw, hq, hf, hi, hg) = jnp.split(h @ w_in_l, cuts, axis=-1)
    heads = lambda a, n: a.reshape(B, T, n, -1)
    return (heads(fq, H_FOX), heads(fk, H_FOX), heads(fv, H_FOX), ff,
            heads(dq, H_DSA), heads(dk, H_DSA), heads(dv, H_DSA),
            heads(iq, N_IDX_HEADS), ik, iw, hq, hf, hi, hg)


def gather_pages(pool, page_table):
    g = pool[page_table]
    return g.reshape((page_table.shape[0], page_table.shape[1] * pool.shape[1]) + pool.shape[2:])


def gather_rows(pool, page_table, idx):
    B = idx.shape[0]
    phys = jnp.take_along_axis(page_table, (idx // PAGE_SIZE).reshape(B, -1), axis=1).reshape(idx.shape)
    return pool[phys, idx % PAGE_SIZE]


def fox_attend(q, k, v, fq, fk, qpos, kpos):
    s = jnp.einsum('bqhd,bkhd->bhqk', q, k).astype(F32) * HEAD_DIM ** -0.5
    s = s + jnp.transpose(fq, (0, 2, 1))[..., None] - jnp.transpose(fk, (0, 2, 1))[:, :, None, :]
    s = jnp.where(kpos[None, None, None, :] <= qpos[None, None, :, None], s, -jnp.inf)
    p = jax.nn.softmax(s, axis=-1)
    return jnp.einsum('bhqk,bkhd->bqhd', p.astype(v.dtype), v)


def fox_prompt(q, k, v, logf):
    B, T, H, Dh = q.shape
    F = jnp.cumsum(logf, axis=1)
    kpos = jnp.arange(T)

    def block(i):
        s0 = i * Q_BLOCK
        qb = lax.dynamic_slice_in_dim(q, s0, Q_BLOCK, axis=1)
        fb = lax.dynamic_slice_in_dim(F, s0, Q_BLOCK, axis=1)
        return fox_attend(qb, k, v, fb, F, s0 + jnp.arange(Q_BLOCK), kpos)

    out = lax.map(block, jnp.arange(T // Q_BLOCK))
    return jnp.moveaxis(out, 0, 1).reshape(B, T, H, Dh)


def dsa_attend(q, iq, iw, ik_all, qpos, gather):
    n_keys = ik_all.shape[1]
    k_sel = min(TOPK_MAX, n_keys // 4)
    rel = jax.nn.relu(jnp.einsum('bqhe,bke->bqhk', iq, ik_all).astype(F32) * IDX_DIM ** -0.5)
    score = jnp.einsum('bqh,bqhk->bqk', iw.astype(F32) * N_IDX_HEADS ** -0.5, rel)
    kpos = jnp.arange(n_keys)
    score = jnp.where(kpos[None, None, :] <= qpos[None, :, None], score, -jnp.inf)
    _, idx = lax.top_k(score, k_sel)
    valid = idx <= qpos[None, :, None]
    ks, vs = gather(idx)
    s = jnp.einsum('bqhd,bqkhd->bqhk', q, ks).astype(F32) * HEAD_DIM ** -0.5
    s = jnp.where(valid[:, :, None, :], s, -jnp.inf)
    p = jax.nn.softmax(s, axis=-1)
    return jnp.einsum('bqhk,bqkhd->bqhd', p.astype(vs.dtype), vs)


def dsa_prompt(q, k, v, iq, ik, iw):
    B, T, H, Dh = q.shape
    take = jax.vmap(lambda a, i: a[i])

    def gather(idx):
        return take(k, idx), take(v, idx)

    def block(i):
        s0 = i * Q_BLOCK
        sl = lambda a: lax.dynamic_slice_in_dim(a, s0, Q_BLOCK, axis=1)
        return dsa_attend(sl(q), sl(iq), sl(iw), ik, s0 + jnp.arange(Q_BLOCK), gather)

    out = lax.map(block, jnp.arange(T // Q_BLOCK))
    return jnp.moveaxis(out, 0, 1).reshape(B, T, H, Dh)


def hgrn_chunked(q, logf, k, v, S0):
    B, T, H, DK = q.shape
    DV = v.shape[-1]
    C = min(HG_CHUNK, T)
    n = -(-T // C)
    pad = n * C - T
    if pad:
        pw = ((0, 0), (0, pad), (0, 0), (0, 0))
        q, logf, k, v = [jnp.pad(a, pw) for a in (q, logf, k, v)]

    def chunks(a):
        return a.reshape(B, n, C, H, a.shape[-1]).transpose(1, 0, 3, 2, 4)

    tri = jnp.tril(jnp.ones((C, C), dtype=bool))

    def step(S, inp):
        qc, gc, kc, vc = inp
        G = jnp.cumsum(gc, axis=2)
        inter = jnp.einsum('bhtd,bhde->bhte', qc * jnp.exp(G), S)
        diff = G[:, :, :, None, :] - G[:, :, None, :, :]
        dec = jnp.exp(jnp.where(tri[:, :, None], diff, -jnp.inf))
        A = jnp.einsum('bhtd,bhtsd,bhsd->bhts', qc, dec, kc)
        o = inter + jnp.einsum('bhts,bhse->bhte', A, vc)
        GL = G[:, :, -1:, :]
        S = jnp.exp(GL[:, :, 0, :])[..., None] * S + jnp.einsum('bhsd,bhse->bhde', kc * jnp.exp(GL - G), vc)
        return S, o

    S, o = lax.scan(step, S0, (chunks(q), chunks(logf), chunks(k), chunks(v)))
    o = o.transpose(1, 0, 3, 2, 4).reshape(B, n * C, H, DV)[:, :T]
    return o, S


def hgrn_mix(hq, hf, hi, hg, lb_l, hgn_l, S0):
    B, T, _ = hq.shape
    q = hq.astype(F32).reshape(B, T, H_HG, HG_DK)
    fl = hf.astype(F32).reshape(B, T, H_HG, HG_DK)
    lb = lb_l.astype(F32).reshape(H_HG, HG_DK)
    logf = jnp.logaddexp(jnp.log(lb), jnp.log1p(-lb) + jax.nn.log_sigmoid(fl))
    kk = (1.0 - lb) * jax.nn.sigmoid(-fl)
    v = hi.astype(F32).reshape(B, T, H_HG, HG_DV)
    o, S = hgrn_chunked(q, logf, kk, v, S0.astype(F32))
    o = o * lax.rsqrt(jnp.mean(jnp.square(o), axis=-1, keepdims=True) + LN_EPS)
    o = o.reshape(B, T, HG_VW).astype(hg.dtype) * hgn_l * jax.nn.silu(hg)
    return o, S


def hier_moe(x, wg, bg, we, be, w1, w3, w2):
    shp = x.shape
    xt = x.reshape(-1, shp[-1])
    T = xt.shape[0]
    g_prob = jax.nn.softmax((xt @ wg).astype(F32) + bg, axis=-1)
    p_top, grp = lax.top_k(g_prob, 1)
    e_logits = ((xt @ we).astype(F32) + be).reshape(T, N_GROUPS, EXPERTS_PER_GROUP)
    e_in = jnp.take_along_axis(e_logits, grp[:, :, None], axis=1)[:, 0]
    v2, j2 = lax.top_k(e_in, TOP_K_INNER)
    gate = jax.nn.softmax(v2, axis=-1) * p_top
    eid = (grp * EXPERTS_PER_GROUP + j2).reshape(-1)
    tok = jnp.repeat(jnp.arange(T, dtype=jnp.int32), TOP_K_INNER)
    wgt = gate.reshape(-1)
    order = jnp.argsort(eid)
    eid_s, tok_s, w_s = eid[order], tok[order], wgt[order]
    counts = jnp.bincount(eid, length=N_EXPERTS)
    start = jnp.cumsum(counts) - counts
    padded = (counts + MOE_BLOCK - 1) // MOE_BLOCK * MOE_BLOCK
    pad_start = jnp.cumsum(padded) - padded
    pad_end = pad_start + padded
    A = eid.shape[0]
    dest = pad_start[eid_s] + (jnp.arange(A) - start[eid_s])
    n_blocks = -(-(A + N_EXPERTS * (MOE_BLOCK - 1)) // MOE_BLOCK)
    P = n_blocks * MOE_BLOCK
    row_tok = jnp.zeros((P,), jnp.int32).at[dest].set(tok_s)
    row_w = jnp.zeros((P,), F32).at[dest].set(w_s)
    blk_e = jnp.minimum(jnp.searchsorted(pad_end, jnp.arange(n_blocks) * MOE_BLOCK, side='right'), N_EXPERTS - 1)

    def run(args):
        e, toks, ws = args
        xb = xt[toks]
        h = jax.nn.silu(xb @ w1[e]) * (xb @ w3[e])
        return (h @ w2[e]) * ws[:, None].astype(xt.dtype)

    out = lax.map(run, (blk_e, row_tok.reshape(n_blocks, MOE_BLOCK), row_w.reshape(n_blocks, MOE_BLOCK)))
    y = jnp.zeros_like(xt).at[row_tok].add(out.reshape(P, shp[-1]))
    return y.reshape(shp)


def mixers_prompt(h, w_in_l, b_fox_l, lb_l, hgn_l):
    B, T, _ = h.shape
    fq, fk, fv, ff, dq, dk, dv, iq, ik, iw, hq, hf, hi, hg = project(h, w_in_l)
    pos = jnp.arange(T)
    fox_logf = jax.nn.log_sigmoid(ff.astype(F32) + b_fox_l)
    fo = fox_prompt(fq, fk, fv, fox_logf)
    dq = rope_partial(dq, pos)
    dk = rope_partial(dk, pos)
    iq = rope_partial(iq, pos)
    ik = rope_partial(ik[:, :, None, :], pos)[:, :, 0]
    do = dsa_prompt(dq, dk, dv, iq, ik, iw)
    S0 = jnp.zeros((B, H_HG, HG_DK, HG_DV), F32)
    ho, S = hgrn_mix(hq, hf, hi, hg, lb_l, hgn_l, S0)
    mix = jnp.concatenate([fo.reshape(B, T, FOX_W), do.reshape(B, T, DSA_W), ho], axis=-1)
    return mix, (fk, fv, fox_logf, dk, dv, ik, S)


def mixers_sample(h, page_table, ck_f, cv_f, clf, ck_d, cv_d, cik, S0, w_in_l, b_fox_l, lb_l, hgn_l):
    B, T, _ = h.shape
    fq, fk, fv, ff, dq, dk, dv, iq, ik, iw, hq, hf, hi, hg = project(h, w_in_l)
    past_len = page_table.shape[1] * PAGE_SIZE
    qpos = past_len + jnp.arange(T)
    kpos = jnp.arange(past_len + T)
    fox_logf = jax.nn.log_sigmoid(ff.astype(F32) + b_fox_l)
    kf_all = jnp.concatenate([gather_pages(ck_f, page_table), fk], axis=1)
    vf_all = jnp.concatenate([gather_pages(cv_f, page_table), fv], axis=1)
    F = jnp.cumsum(jnp.concatenate([gather_pages(clf, page_table).astype(F32), fox_logf], axis=1), axis=1)
    fo = fox_attend(fq, kf_all, vf_all, F[:, past_len:], F, qpos, kpos)
    dq = rope_partial(dq, qpos)
    dk = rope_partial(dk, qpos)
    iq = rope_partial(iq, qpos)
    ik = rope_partial(ik[:, :, None, :], qpos)[:, :, 0]
    ik_all = jnp.concatenate([gather_pages(cik, page_table), ik], axis=1)
    take = jax.vmap(lambda a, i: a[i])

    def gather(idx):
        in_past = (idx < past_len)[..., None, None]
        ip = jnp.minimum(idx, past_len - 1)
        inew = jnp.clip(idx - past_len, 0, T - 1)
        return (jnp.where(in_past, gather_rows(ck_d, page_table, ip), take(dk, inew)),
                jnp.where(in_past, gather_rows(cv_d, page_table, ip), take(dv, inew)))

    do = dsa_attend(dq, iq, iw, ik_all, qpos, gather)
    ho, S = hgrn_mix(hq, hf, hi, hg, lb_l, hgn_l, S0)
    mix = jnp.concatenate([fo.reshape(B, T, FOX_W), do.reshape(B, T, DSA_W), ho], axis=-1)
    return mix, (fk, fv, fox_logf, dk, dv, ik, S)


def setup_inputs(seed: int = 0) -> dict:
    key = jax.random.key(seed)
    ks = jax.random.split(key, 32)
    n_pages = PAST_LEN // PAGE_SIZE
    n_used = DEC_BATCH * n_pages
    n_pool = n_used + n_used // 4
    nrm = lambda k, shp: jax.random.normal(k, shp, F32)
    page_table = jax.random.permutation(ks[0], n_pool)[:n_used].reshape(DEC_BATCH, n_pages).astype(jnp.int32)
    return {
        'x_prompt': nrm(ks[1], (BATCH, SEQ, D_MODEL)),
        'x_sample': nrm(ks[2], (DEC_BATCH, DEC_SEQ, D_MODEL)),
        'cache_fox_k': nrm(ks[3], (DEPTH, n_pool, PAGE_SIZE, H_FOX, HEAD_DIM)),
        'cache_fox_v': nrm(ks[4], (DEPTH, n_pool, PAGE_SIZE, H_FOX, HEAD_DIM)),
        'cache_fox_logf': jax.nn.log_sigmoid(FOX_GATE_BIAS + nrm(ks[5], (DEPTH, n_pool, PAGE_SIZE, H_FOX))),
        'cache_dsa_k': nrm(ks[6], (DEPTH, n_pool, PAGE_SIZE, H_DSA, HEAD_DIM)),
        'cache_dsa_v': nrm(ks[7], (DEPTH, n_pool, PAGE_SIZE, H_DSA, HEAD_DIM)),
        'cache_idx_k': nrm(ks[8], (DEPTH, n_pool, PAGE_SIZE, IDX_DIM)),
        'state_hgrn': 0.5 * nrm(ks[9], (DEPTH, DEC_BATCH, H_HG, HG_DK, HG_DV)),
        'page_table': page_table,
        'w_in': nrm(ks[10], (DEPTH, D_MODEL, N_IN)) * D_MODEL ** -0.5,
        'b_fox': FOX_GATE_BIAS + 0.1 * nrm(ks[11], (DEPTH, H_FOX)),
        'hg_lb': nrm(ks[12], (DEPTH, HG_KW)),
        'hg_norm': 1.0 + 0.01 * nrm(ks[13], (DEPTH, HG_VW)),
        'w_out': nrm(ks[14], (DEPTH, D_MIX, D_MODEL)) * (D_MIX ** -0.5 * DEEPNORM_BETA),
        'ln1_g': 1.0 + 0.01 * nrm(ks[15], (DEPTH, D_MODEL)),
        'ln1_b': 0.01 * nrm(ks[16], (DEPTH, D_MODEL)),
        'moe_wg': nrm(ks[17], (DEPTH, D_MODEL, N_GROUPS)) * D_MODEL ** -0.5,
        'moe_bg': 0.01 * nrm(ks[18], (DEPTH, N_GROUPS)),
        'moe_we': nrm(ks[19], (DEPTH, D_MODEL, N_EXPERTS)) * D_MODEL ** -0.5,
        'moe_be': 0.01 * nrm(ks[20], (DEPTH, N_EXPERTS)),
        'moe_w1': nrm(ks[21], (DEPTH, N_EXPERTS, D_MODEL, D_FF_EXPERT)) * D_MODEL ** -0.5,
        'moe_w3': nrm(ks[22], (DEPTH, N_EXPERTS, D_MODEL, D_FF_EXPERT)) * D_MODEL ** -0.5,
        'moe_w2': nrm(ks[23], (DEPTH, N_EXPERTS, D_FF_EXPERT, D_MODEL)) * (D_FF_EXPERT ** -0.5 * DEEPNORM_BETA),
        'ln2_g': 1.0 + 0.01 * nrm(ks[24], (DEPTH, D_MODEL)),
        'ln2_b': 0.01 * nrm(ks[25], (DEPTH, D_MODEL)),
    }


def reference(x_prompt, x_sample, cache_fox_k, cache_fox_v, cache_fox_logf, cache_dsa_k, cache_dsa_v, cache_idx_k,
              state_hgrn, page_table, w_in, b_fox, hg_lb, hg_norm, w_out, ln1_g, ln1_b,
              moe_wg, moe_bg, moe_we, moe_be, moe_w1, moe_w3, moe_w2, ln2_g, ln2_b):
    lb_all = jnp.cumsum(jax.nn.softmax(hg_lb.astype(F32), axis=0), axis=0)
    lb_all = lb_all - lb_all[0:1]
    xp, xs = x_prompt, x_sample
    new_p = [[] for _ in range(7)]
    new_s = [[] for _ in range(7)]
    for l in range(DEPTH):
        mix_p, st_p = mixers_prompt(xp, w_in[l], b_fox[l], lb_all[l], hg_norm[l])
        mix_s, st_s = mixers_sample(xs, page_table, cache_fox_k[l], cache_fox_v[l], cache_fox_logf[l],
                                    cache_dsa_k[l], cache_dsa_v[l], cache_idx_k[l], state_hgrn[l],
                                    w_in[l], b_fox[l], lb_all[l], hg_norm[l])
        xp = post_norm(xp, mix_p @ w_out[l], ln1_g[l], ln1_b[l])
        xs = post_norm(xs, mix_s @ w_out[l], ln1_g[l], ln1_b[l])
        xp = post_norm(xp, hier_moe(xp, moe_wg[l], moe_bg[l], moe_we[l], moe_be[l], moe_w1[l], moe_w3[l], moe_w2[l]), ln2_g[l], ln2_b[l])
        xs = post_norm(xs, hier_moe(xs, moe_wg[l], moe_bg[l], moe_we[l], moe_be[l], moe_w1[l], moe_w3[l], moe_w2[l]), ln2_g[l], ln2_b[l])
        for lst, a in zip(new_p, st_p):
            lst.append(a)
        for lst, a in zip(new_s, st_s):
            lst.append(a)
    fox_k_p, fox_v_p, fox_logf_p, dsa_k_p, dsa_v_p, idx_k_p, hgrn_state_p = [jnp.stack(a) for a in new_p]
    fox_k_s, fox_v_s, fox_logf_s, dsa_k_s, dsa_v_s, idx_k_s, hgrn_state_s = [jnp.stack(a) for a in new_s]
    return (xp, xs, fox_k_p, fox_v_p, fox_logf_p, dsa_k_p, dsa_v_p, idx_k_p, hgrn_state_p,
            fox_k_s, fox_v_s, fox_logf_s, dsa_k_s, dsa_v_s, idx_k_s, hgrn_state_s)
```

```python
import functools

import numpy as np
import jax
import jax.numpy as jnp
from jax import lax
from jax.experimental import pallas as pl
from jax.experimental.pallas import tpu as pltpu

F32 = jnp.float32
BF16 = jnp.bfloat16
I32 = jnp.int32

D_MODEL = 1024
DEPTH = 4
HEAD_DIM = 64
H_FOX = 4
H_DSA = 4
H_HG = 8
HG_D = 64
N_IDX_HEADS = 8
IDX_DIM = 32
PAGE_SIZE = 128
TOPK_MAX = 256
ROPE_THETA = 500000.0
N_GROUPS = 4
EPG = 4
N_EXPERTS = 16
D_FF = 512
LN_EPS = 1e-5
ALPHA = (2 * DEPTH) ** 0.25
HW = 256
HGW = 512

C_FQ, C_FK, C_FV, C_DQ, C_DK, C_DV, C_IQ, C_IKT = (i * 256 for i in range(8))
C_HQ, C_HF, C_HI, C_HG = 2048, 2560, 3072, 3584
C_SM = 4096
N_P = 4224
SM_IK = 32
SM_IW = 64

VMEM_LIMIT = 56 * 1024 * 1024
NEG_INF = float("-inf")


def _cparams(sem):
    return pltpu.CompilerParams(dimension_semantics=sem, vmem_limit_bytes=VMEM_LIMIT)


def _iota(shape, dim):
    return lax.broadcasted_iota(I32, shape, dim)


def _dot(a, b):
    return jnp.dot(a, b, preferred_element_type=F32)


def _dot_nt(a, b):
    return lax.dot_general(a, b, (((1,), (1,)), ((), ())), preferred_element_type=F32)


def _split3(x):
    h = x.astype(BF16)
    r = x - h.astype(F32)
    m = r.astype(BF16)
    l = (r - m.astype(F32)).astype(BF16)
    return h, m, l


def _dot3_l(x, w):
    h, m, l = _split3(x)
    return _dot(h, w) + _dot(m, w) + _dot(l, w)


def _dot3_r(w, x):
    h, m, l = _split3(x)
    return _dot(w, h) + _dot(w, m) + _dot(w, l)


def _layer_norm(y, g, b):
    mu = jnp.mean(y, axis=-1, keepdims=True)
    yc = y - mu
    var = jnp.mean(yc * yc, axis=-1, keepdims=True)
    return yc * lax.rsqrt(var + LN_EPS) * g + b


def _lb_kernel(x_ref, lb_ref, llb_ref, l1m_ref, om_ref):
    x = x_ref[...]
    m = jnp.max(x, axis=0, keepdims=True)
    e = jnp.exp(x - m)
    p = e / jnp.sum(e, axis=0, keepdims=True)
    n = x.shape[0]
    c = p[0:1]
    c0 = c
    for l in range(n):
        if l > 0:
            c = c + p[l:l + 1]
        lb = c - c0
        lb_ref[l:l + 1, :] = lb
        llb_ref[l:l + 1, :] = jnp.log(lb)
        l1m_ref[l:l + 1, :] = jnp.log1p(-lb)
        om_ref[l:l + 1, :] = 1.0 - lb


def lb_params(hg_lb):
    shp = jax.ShapeDtypeStruct(hg_lb.shape, F32)
    return pl.pallas_call(_lb_kernel, out_shape=(shp, shp, shp, shp))(hg_lb.astype(F32))


def _rope(v, t_ref, half):
    n = v.shape[-1]
    return (v * t_ref[0] + pltpu.roll(v, n - half, 1) * t_ref[1]
            + pltpu.roll(v, half, 1) * t_ref[2])


def _proj_kernel(x_ref, w_ref, t64_ref, t32_ref, ts_ref, bf_ref, o_ref):
    x = x_ref[...].astype(BF16)

    def seg(a, wd):
        return _dot(x, w_ref[:, a:a + wd])

    o_ref[:, C_FQ:C_FQ + 768] = seg(C_FQ, 768)
    o_ref[:, C_DQ:C_DQ + HW] = _rope(seg(C_DQ, HW), t64_ref, 8)
    o_ref[:, C_DK:C_DK + HW] = _rope(seg(C_DK, HW), t64_ref, 8)
    o_ref[:, C_DV:C_DV + HW] = seg(C_DV, HW)
    o_ref[:, C_IQ:C_IQ + HW] = _rope(seg(C_IQ, HW), t32_ref, 4)
    o_ref[:, C_IKT:C_IKT + HW] = _rope(seg(C_IKT, HW), t32_ref, 4)
    o_ref[:, C_HQ:C_HQ + 1024] = seg(C_HQ, 1024)
    o_ref[:, C_HI:C_HI + 1024] = seg(C_HI, 1024)
    sm = _rope(seg(C_SM, 128), ts_ref, 4)
    z = sm + bf_ref[...]
    logsig = jnp.minimum(z, 0.0) - jnp.log1p(jnp.exp(-jnp.abs(z)))
    o_ref[:, C_SM:C_SM + 128] = jnp.where(_iota(sm.shape, 1) < H_FOX, logsig, sm)


def project(x_all, w_p, t64, t32, ts, bfox_row, tm, tiles_per_seq, n_seq_tiles):
    n = x_all.shape[0]
    nt = n // tm

    def tab_map(i):
        return (0, jnp.where(i < n_seq_tiles, i % tiles_per_seq, tiles_per_seq), 0)

    return pl.pallas_call(
        _proj_kernel,
        grid=(nt,),
        in_specs=[
            pl.BlockSpec((tm, D_MODEL), lambda i: (i, 0)),
            pl.BlockSpec((D_MODEL, N_P), lambda i: (0, 0)),
            pl.BlockSpec((3, tm, HW), tab_map),
            pl.BlockSpec((3, tm, HW), tab_map),
            pl.BlockSpec((3, tm, 128), tab_map),
            pl.BlockSpec((1, 128), lambda i: (0, 0)),
        ],
        out_specs=pl.BlockSpec((tm, N_P), lambda i: (i, 0)),
        out_shape=jax.ShapeDtypeStruct((n, N_P), F32),
        compiler_params=_cparams(("parallel",)),
    )(x_all, w_p, t64, t32, ts, bfox_row)


def rope_tables(seq, tm):
    pos = jnp.concatenate([jnp.arange(seq), jnp.full((tm,), seq)]).astype(F32)

    def head_tabs(hd):
        rot = hd // 4
        half = rot // 2
        inv = ROPE_THETA ** (-jnp.arange(half, dtype=F32) * 2.0 / rot)
        ang = pos[:, None] * inv[None, :]
        cos, sin = jnp.cos(ang), jnp.sin(ang)
        n = pos.shape[0]
        one = jnp.ones((n, hd - rot), F32)
        zero = jnp.zeros((n, hd - rot), F32)
        zh = jnp.zeros((n, half), F32)
        c = jnp.concatenate([cos, cos, one], -1)
        sa = jnp.concatenate([-sin, zh, zero], -1)
        sb = jnp.concatenate([zh, sin, zero], -1)
        return c, sa, sb

    t64 = jnp.stack([jnp.tile(a, (1, HW // 64)) for a in head_tabs(64)])
    h32 = head_tabs(32)
    t32 = jnp.stack([jnp.tile(a, (1, HW // 32)) for a in h32])
    n = pos.shape[0]
    fill = [jnp.ones, jnp.zeros, jnp.zeros]
    ts = jnp.stack([jnp.concatenate([f((n, SM_IK), F32), a, f((n, 128 - SM_IK - 32), F32)], -1)
                    for f, a in zip(fill, h32)])
    return t64, t32, ts


def permute_w_in(w_in):
    L = w_in.shape[0]
    z = lambda n: jnp.zeros((L, D_MODEL, n), w_in.dtype)
    ik = w_in[:, :, 1796:1828]
    small = jnp.concatenate([w_in[:, :, 768:772], z(SM_IK - 4), ik, w_in[:, :, 1828:1836],
                             z(128 - SM_IW - 8)], -1)
    w = jnp.concatenate([w_in[:, :, 0:768], w_in[:, :, 772:1540], w_in[:, :, 1540:1796],
                         jnp.tile(ik, (1, 1, 8)), w_in[:, :, 1836:3884], small], -1)
    return w.astype(BF16)


def _cumsum_kernel(x_ref, tri_ref, o_ref):
    r, n = x_ref.shape
    carry = jnp.zeros((r, 1), F32)
    tri = tri_ref[...]
    for c in range(n // 128):
        blk = x_ref[:, c * 128:(c + 1) * 128]
        cs = _dot3_l(blk, tri) + carry
        o_ref[:, c * 128:(c + 1) * 128] = cs
        carry = carry + jnp.sum(blk, axis=-1, keepdims=True)


def cumsum_rows(x):
    tri = jnp.asarray(np.triu(np.ones((128, 128), np.float32)), BF16)
    return pl.pallas_call(_cumsum_kernel, out_shape=jax.ShapeDtypeStruct(x.shape, F32))(x, tri)


def _fox_prompt_kernel(q_ref, k_ref, v_ref, f_ref, o_ref, qm_ref, acc_ref, *, tq):
    qi = pl.program_id(1)
    q = q_ref[...] * (HEAD_DIM ** -0.5)
    hl = _iota(q.shape, 1) >> 6
    for h in range(H_FOX):
        qm_ref[h] = jnp.where(hl == h, q, 0.0).astype(BF16)
    acc_ref[...] = jnp.zeros(acc_ref.shape, F32)

    def step(j, carry, diag):
        ms, ls = carry
        start = pl.multiple_of(j * tq, tq)
        kb = k_ref[pl.ds(start, tq), :].astype(BF16)
        vb = v_ref[pl.ds(start, tq), :].astype(BF16)
        new_m, new_l = [], []
        for h in range(H_FOX):
            s = _dot_nt(qm_ref[h], kb) - f_ref[0, j][h:h + 1, :]
            if diag:
                s = jnp.where(_iota(s.shape, 1) <= _iota(s.shape, 0), s, NEG_INF)
            m = jnp.maximum(ms[h], jnp.max(s, axis=-1, keepdims=True))
            p = jnp.exp(s - m)
            a = jnp.exp(ms[h] - m)
            new_l.append(a * ls[h] + jnp.sum(p, axis=-1, keepdims=True))
            acc_ref[h] = a * acc_ref[h] + _dot(p.astype(BF16), vb)
            new_m.append(m)
        return tuple(new_m), tuple(new_l)

    init = (tuple(jnp.full((tq, 1), NEG_INF, F32) for _ in range(H_FOX)),
            tuple(jnp.zeros((tq, 1), F32) for _ in range(H_FOX)))
    carry = lax.fori_loop(0, qi, lambda j, c: step(j, c, False), init)
    ms, ls = step(qi, carry, True)
    out = jnp.zeros((tq, HW), F32)
    for h in range(H_FOX):
        out = out + jnp.where(hl == h, acc_ref[h] / ls[h], 0.0)
    o_ref[...] = out


def fox_prompt(p_all, fcum, nb, seq, tq):
    nq = seq // tq
    return pl.pallas_call(
        functools.partial(_fox_prompt_kernel, tq=tq),
        grid=(nb, nq),
        in_specs=[
            pl.BlockSpec((tq, HW), lambda b, i: (b * nq + i, C_FQ // HW)),
            pl.BlockSpec((seq, HW), lambda b, i: (b, C_FK // HW)),
            pl.BlockSpec((seq, HW), lambda b, i: (b, C_FV // HW)),
            pl.BlockSpec((1, nq, 8, tq), lambda b, i: (b, 0, 0, 0)),
        ],
        out_specs=pl.BlockSpec((tq, HW), lambda b, i: (b * nq + i, 0)),
        out_shape=jax.ShapeDtypeStruct((nb * seq, HW), F32),
        scratch_shapes=[pltpu.VMEM((H_FOX, tq, HW), BF16), pltpu.VMEM((H_FOX, tq, HW), F32)],
        compiler_params=_cparams(("parallel", "parallel")),
    )(p_all, p_all, p_all, fcum)


def _sortable_key(score):
    bits = lax.bitcast_convert_type(score + 0.0, I32)
    return jnp.where(bits < 0, bits ^ jnp.int32(0x7FFFFFFF), bits)


def _select_topk(key_ref, eq_ref, vis, k):
    rows, cols = key_ref.shape
    kf = float(k)

    def body(i, thr):
        cand = thr + jnp.left_shift(jnp.int32(1), 31 - i)
        cnt = jnp.sum(jnp.where(key_ref[...] >= cand, 1.0, 0.0), axis=-1, keepdims=True)
        return jnp.where(cnt >= kf, cand, thr)

    thr = lax.fori_loop(0, 32, body, jnp.full((rows, 1), -2 ** 31, I32))
    key = key_ref[...]
    gt = key > thr
    eq_ref[...] = jnp.where(key == thr, 1.0, 0.0)
    need = kf - jnp.sum(jnp.where(gt, 1.0, 0.0), axis=-1, keepdims=True)
    nbits = int(cols).bit_length()

    def body2(i, x):
        cand = x + jnp.left_shift(jnp.int32(1), nbits - 1 - i)
        col = _iota((rows, cols), 1)
        c = jnp.sum(jnp.where(col < cand, eq_ref[...], 0.0), axis=-1, keepdims=True)
        return jnp.where(c < need, cand, x)

    x = lax.fori_loop(0, nbits, body2, jnp.zeros((rows, 1), I32))
    col = _iota((rows, cols), 1)
    return vis & (gt | ((eq_ref[...] > 0.0) & (col <= x)))


def _dsa_prompt_kernel(dq_ref, dk_ref, dv_ref, iq_ref, ikt_ref, sm_ref, o_ref, key_ref, eq_ref,
                       *, tq, k_sel):
    qi = pl.program_id(1)
    seq = dk_ref.shape[0]
    iq = iq_ref[...] * (IDX_DIM ** -0.5)
    ikt = ikt_ref[...].astype(BF16)
    sm = sm_ref[...]
    il = _iota(iq.shape, 1) >> 5
    score = jnp.zeros((tq, seq), F32)
    for h in range(N_IDX_HEADS):
        rel = jnp.maximum(_dot_nt(jnp.where(il == h, iq, 0.0).astype(BF16), ikt), 0.0)
        w = sm[:, SM_IW + h:SM_IW + h + 1] * (N_IDX_HEADS ** -0.5)
        score = score + w * rel
    vis = _iota((tq, seq), 1) <= _iota((tq, seq), 0) + qi * tq
    key_ref[...] = _sortable_key(jnp.where(vis, score, NEG_INF))
    sel = _select_topk(key_ref, eq_ref, vis, k_sel)

    dq = dq_ref[...] * (HEAD_DIM ** -0.5)
    dk = dk_ref[...].astype(BF16)
    dv = dv_ref[...].astype(BF16)
    hl = _iota(dq.shape, 1) >> 6
    out = jnp.zeros((tq, HW), F32)
    for h in range(H_DSA):
        s = _dot_nt(jnp.where(hl == h, dq, 0.0).astype(BF16), dk)
        s = jnp.where(sel, s, NEG_INF)
        m = jnp.max(s, axis=-1, keepdims=True)
        p = jnp.exp(s - m)
        l = jnp.sum(p, axis=-1, keepdims=True)
        out = out + jnp.where(hl == h, _dot(p.astype(BF16), dv) / l, 0.0)
    o_ref[...] = out


def dsa_prompt(p_all, nb, seq, tq):
    nq = seq // tq
    k_sel = min(TOPK_MAX, seq // 4)
    return pl.pallas_call(
        functools.partial(_dsa_prompt_kernel, tq=tq, k_sel=k_sel),
        grid=(nb, nq),
        in_specs=[
            pl.BlockSpec((tq, HW), lambda b, i: (b * nq + i, C_DQ // HW)),
            pl.BlockSpec((seq, HW), lambda b, i: (b, C_DK // HW)),
            pl.BlockSpec((seq, HW), lambda b, i: (b, C_DV // HW)),
            pl.BlockSpec((tq, HW), lambda b, i: (b * nq + i, C_IQ // HW)),
            pl.BlockSpec((seq, HW), lambda b, i: (b, C_IKT // HW)),
            pl.BlockSpec((tq, 128), lambda b, i: (b * nq + i, C_SM // 128)),
        ],
        out_specs=pl.BlockSpec((tq, HW), lambda b, i: (b * nq + i, 0)),
        out_shape=jax.ShapeDtypeStruct((nb * seq, HW), F32),
        scratch_shapes=[pltpu.VMEM((tq, seq), I32), pltpu.VMEM((tq, seq), F32)],
        compiler_params=_cparams(("parallel", "parallel")),
    )(p_all, p_all, p_all, p_all, p_all, p_all)


def hgrn_level_mats(c):
    nl = int(np.log2(c))
    t = np.arange(c)
    mats = [(t[None, :] <= t[:, None]).astype(np.float32)]
    masks = []
    for L in range(nl):
        bit = (t >> L) & 1
        lo = (t >> L) << L
        w = np.zeros((c, c), np.float32)
        for r in range(c):
            if bit[r]:
                w[r, lo[r]:r + 1] = 1.0
            else:
                w[r, r + 1:lo[r] + (1 << L)] = 1.0
        mats.append(w)
        same = (t[:, None] >> (L + 1)) == (t[None, :] >> (L + 1))
        masks.append((bit[:, None] == 1) & (bit[None, :] == 0) & same)
    return np.stack(mats), np.stack(masks).astype(np.float32)


def _hgrn_gates(fl, llb, l1m, om):
    ls = jnp.minimum(fl, 0.0) - jnp.log1p(jnp.exp(-jnp.abs(fl)))
    b = l1m + ls
    logf = jnp.maximum(llb, b) + jnp.log1p(jnp.exp(-jnp.abs(llb - b)))
    kk = om / (1.0 + jnp.exp(fl))
    return logf, kk


def _hgrn_prompt_kernel(q_ref, f_ref, v_ref, g_ref, par_ref, wl_ref, ml_ref, bd_ref, o_ref, st_ref,
                        s_scr, *, c, nl):
    ci = pl.program_id(1)

    @pl.when(ci == 0)
    def _():
        s_scr[...] = jnp.zeros(s_scr.shape, F32)

    q = q_ref[...]
    v = v_ref[...]
    logf, kk = _hgrn_gates(f_ref[...], par_ref[0:1, :], par_ref[1:2, :], par_ref[2:3, :])
    lf3 = jnp.concatenate(_split3(logf), axis=-1)

    def rowsum(i):
        e = _dot(wl_ref[i], lf3)
        return e[:, 0:HGW] + e[:, HGW:2 * HGW] + e[:, 2 * HGW:3 * HGW]

    bd = bd_ref[...]
    vb = v.astype(BF16)
    row = _iota((c, HGW), 0)
    hl = _iota((c, HW), 1) >> 6
    a_heads = [jnp.zeros((c, c), F32) for _ in range(H_HG)]
    for L in range(nl):
        e = rowsum(1 + L)
        x = (jnp.where(((row >> L) & 1) == 1, q, kk) * jnp.exp(e)).astype(BF16)
        msk = ml_ref[L] > 0.0
        for h in range(H_HG):
            xh = x[:, (h // 4) * HW:(h // 4 + 1) * HW]
            lhs = jnp.where(hl == (h % 4), xh, jnp.zeros_like(xh))
            a_heads[h] = a_heads[h] + jnp.where(msk, _dot_nt(lhs, xh), 0.0)
    o = _dot((q * kk).astype(BF16), bd) * v
    intra = []
    for half in range(2):
        acc = jnp.zeros((c, HW), F32)
        vh = vb[:, half * HW:(half + 1) * HW]
        for hh in range(4):
            acc = acc + jnp.where(hl == hh, _dot(a_heads[half * 4 + hh].astype(BF16), vh), 0.0)
        intra.append(acc)
    o = o + jnp.concatenate(intra, axis=-1)
    g = rowsum(0)
    st = s_scr[...]
    o = o + _dot_nt((q * jnp.exp(g)).astype(BF16), st.astype(BF16))
    gl = g[c - 1:c, :]
    kd = (kk * jnp.exp(gl - g)).astype(BF16)
    upd = _dot(v.T.astype(BF16), kd)
    s_new = st * jnp.exp(gl) + jnp.where(bd > 0, upd, 0.0)
    s_scr[...] = s_new
    st_ref[0] = s_new

    o2 = o * o
    o2h = o2.astype(BF16)
    o2l = (o2 - o2h.astype(F32)).astype(BF16)
    ms = (_dot(o2h, bd) + _dot(o2l, bd)) * (1.0 / HG_D)
    hg = g_ref[...]
    o_ref[...] = o * lax.rsqrt(ms + LN_EPS) * par_ref[3:4, :] * (hg / (1.0 + jnp.exp(-hg)))


def hgrn_prompt(p_all, par, nb, seq, c):
    nc = seq // c
    nl = int(np.log2(c))
    wl, ml = hgrn_level_mats(c)
    bd = np.kron(np.eye(H_HG, dtype=np.float32), np.ones((HG_D, HG_D), np.float32))
    o, st = pl.pallas_call(
        functools.partial(_hgrn_prompt_kernel, c=c, nl=nl),
        grid=(nb, nc),
        in_specs=[
            pl.BlockSpec((c, HGW), lambda b, i: (b * nc + i, C_HQ // HGW)),
            pl.BlockSpec((c, HGW), lambda b, i: (b * nc + i, C_HF // HGW)),
            pl.BlockSpec((c, HGW), lambda b, i: (b * nc + i, C_HI // HGW)),
            pl.BlockSpec((c, HGW), lambda b, i: (b * nc + i, C_HG // HGW)),
            pl.BlockSpec((8, HGW), lambda b, i: (0, 0)),
            pl.BlockSpec((nl + 1, c, c), lambda b, i: (0, 0, 0)),
            pl.BlockSpec((nl, c, c), lambda b, i: (0, 0, 0)),
            pl.BlockSpec((HGW, HGW), lambda b, i: (0, 0)),
        ],
        out_specs=[
            pl.BlockSpec((c, HGW), lambda b, i: (b * nc + i, 0)),
            pl.BlockSpec((1, HGW, HGW), lambda b, i: (b, 0, 0)),
        ],
        out_shape=[jax.ShapeDtypeStruct((nb * seq, HGW), F32),
                   jax.ShapeDtypeStruct((nb, HGW, HGW), F32)],
        scratch_shapes=[pltpu.VMEM((HGW, HGW), F32)],
        compiler_params=_cparams(("parallel", "arbitrary")),
    )(p_all, p_all, p_all, p_all, par, jnp.asarray(wl, BF16), jnp.asarray(ml, F32),
      jnp.asarray(bd, BF16))
    ar = jnp.arange(H_HG)
    st = st.reshape(nb, H_HG, HG_D, H_HG, HG_D)[:, ar, :, ar, :]
    return o, jnp.transpose(st, (1, 0, 3, 2))


def _hgrn_sample_kernel(q_ref, f_ref, v_ref, g_ref, s_ref, pc_ref, n_ref, o_ref, so_ref):
    logf, kk = _hgrn_gates(f_ref[...], pc_ref[0][None], pc_ref[1][None], pc_ref[2][None])
    s_new = jnp.exp(logf) * s_ref[0] + kk * v_ref[...]
    so_ref[...] = s_new
    o = jnp.sum(q_ref[...] * s_new, axis=2, keepdims=True)
    ms = jnp.mean(o * o, axis=-1, keepdims=True)
    hg = g_ref[...]
    o_ref[...] = o * lax.rsqrt(ms + LN_EPS) * n_ref[...][None] * (hg / (1.0 + jnp.exp(-hg)))


def hgrn_sample(qc, fc, vr, gr, state_all, layer, par_col, norm_row, bb):
    ns = qc.shape[0]
    col = pl.BlockSpec((bb, H_HG, HG_D, 1), lambda i: (i, 0, 0, 0))
    rw = pl.BlockSpec((bb, H_HG, 1, HG_D), lambda i: (i, 0, 0, 0))
    return pl.pallas_call(
        _hgrn_sample_kernel,
        grid=(ns // bb,),
        in_specs=[col, col, rw, rw,
                  pl.BlockSpec((1, bb, H_HG, HG_D, HG_D), lambda i: (layer, i, 0, 0, 0)),
                  pl.BlockSpec((3, H_HG, HG_D, 1), lambda i: (0, 0, 0, 0)),
                  pl.BlockSpec((H_HG, 1, HG_D), lambda i: (0, 0, 0))],
        out_specs=[rw, pl.BlockSpec((bb, H_HG, HG_D, HG_D), lambda i: (i, 0, 0, 0))],
        out_shape=[jax.ShapeDtypeStruct((ns, H_HG, 1, HG_D), F32),
                   jax.ShapeDtypeStruct((ns, H_HG, HG_D, HG_D), F32)],
        compiler_params=_cparams(("parallel",)),
    )(qc, fc, vr, gr, state_all, par_col, norm_row)


def _head_rows(q, scale):
    qb = jnp.broadcast_to(q * scale, (8, HW))
    return jnp.where((_iota((8, HW), 1) >> 6) == _iota((8, HW), 0), qb, 0.0)


def _merge_heads(acc):
    return jnp.sum(jnp.where((_iota((8, HW), 1) >> 6) == _iota((8, HW), 0), acc, 0.0),
                   axis=0, keepdims=True)


def _fox_sample_kernel(pt_ref, q_ref, kn_ref, vn_ref, lfn_ref, u_ref, mc_ref, *rest, npages):
    k_refs = rest[0:npages]
    v_refs = rest[npages:2 * npages]
    lf_refs = rest[2 * npages:3 * npages]
    o_ref = rest[3 * npages]
    lf_scr = rest[3 * npages + 1]
    q8 = _head_rows(q_ref[0], HEAD_DIM ** -0.5)
    q8b = q8.astype(BF16)
    lf_scr[...] = jnp.zeros(lf_scr.shape, F32)
    for p in range(npages):
        lf_scr[8 * p:8 * p + H_FOX, :] = lf_refs[p][0, 0]
    lf = lf_scr[...]
    tot = jnp.broadcast_to(jnp.sum(lf, axis=-1, keepdims=True), lf.shape)
    bias = _dot3_l(lf, u_ref[...]) + _dot3_r(mc_ref[...], tot) + lfn_ref[0]
    s_new = jnp.sum(q8 * kn_ref[0], axis=-1, keepdims=True)
    ss = []
    m = s_new
    for p in range(npages):
        s = _dot_nt(q8b, k_refs[p][0, 0].astype(BF16)) + bias[8 * p:8 * p + 8, :]
        ss.append(s)
        m = jnp.maximum(m, jnp.max(s, axis=-1, keepdims=True))
    pn = jnp.exp(s_new - m)
    l = pn
    acc = pn * vn_ref[0]
    for p in range(npages):
        e = jnp.exp(ss[p] - m)
        l = l + jnp.sum(e, axis=-1, keepdims=True)
        acc = acc + _dot(e.astype(BF16), v_refs[p][0, 0].astype(BF16))
    o_ref[0] = _merge_heads(acc / l)


def _page_specs(npages, layer, blk):
    def mk(p):
        return pl.BlockSpec((1, 1) + blk, lambda b, pt: (layer, pt[b, p], 0, 0))
    return [mk(p) for p in range(npages)]


def fox_sample(page_table, q, kn, vn, lfn_col, ck, cv, clf_t, layer):
    ns, npages = page_table.shape
    r = 8 * npages
    u = np.tril(np.ones((PAGE_SIZE, PAGE_SIZE), np.float32), -1)
    pg = np.arange(r) // 8
    hd = np.arange(r) % 8
    mc = ((hd[:, None] == hd[None, :]) & (pg[None, :] > pg[:, None])).astype(np.float32)
    row = pl.BlockSpec((1, 1, HW), lambda b, pt: (b, 0, 0))
    gs = pltpu.PrefetchScalarGridSpec(
        num_scalar_prefetch=1,
        grid=(ns,),
        in_specs=[row, row, row,
                  pl.BlockSpec((1, r, 1), lambda b, pt: (b, 0, 0)),
                  pl.BlockSpec((PAGE_SIZE, PAGE_SIZE), lambda b, pt: (0, 0)),
                  pl.BlockSpec((r, r), lambda b, pt: (0, 0))]
        + _page_specs(npages, layer, (PAGE_SIZE, HW))
        + _page_specs(npages, layer, (PAGE_SIZE, HW))
        + _page_specs(npages, layer, (H_FOX, PAGE_SIZE)),
        out_specs=row,
        scratch_shapes=[pltpu.VMEM((r, PAGE_SIZE), F32)],
    )
    return pl.pallas_call(
        functools.partial(_fox_sample_kernel, npages=npages),
        grid_spec=gs,
        out_shape=jax.ShapeDtypeStruct((ns, 1, HW), F32),
        compiler_params=_cparams(("parallel",)),
    )(page_table, q, kn, vn, lfn_col, jnp.asarray(u, BF16), jnp.asarray(mc, BF16),
      *([ck] * npages), *([cv] * npages), *([clf_t] * npages))


def _dsa_scores_kernel(pt_ref, iq_ref, w_ref, ikn_ref, *rest, npages):
    ik_refs = rest[0:npages]
    o_ref = rest[npages]
    iq = iq_ref[0] * (IDX_DIM ** -0.5)
    w = w_ref[0] * (N_IDX_HEADS ** -0.5)
    iqb = iq.astype(BF16)
    for p in range(npages):
        rel = jnp.maximum(_dot_nt(iqb, ik_refs[p][0, 0].astype(BF16)), 0.0)
        o_ref[0, :, p * PAGE_SIZE:(p + 1) * PAGE_SIZE] = jnp.sum(w * rel, axis=0, keepdims=True)
    rel_n = jnp.maximum(jnp.sum(iq * ikn_ref[0], axis=-1, keepdims=True), 0.0)
    sc_n = jnp.sum(w * rel_n, axis=0, keepdims=True)
    past = npages * PAGE_SIZE
    o_ref[0, :, past:past + PAGE_SIZE] = jnp.where(_iota((1, PAGE_SIZE), 1) == 0, sc_n, NEG_INF)


def dsa_sample_scores(page_table, iq, w_col, ikn, cik, layer):
    ns, npages = page_table.shape
    ncol = (npages + 1) * PAGE_SIZE
    gs = pltpu.PrefetchScalarGridSpec(
        num_scalar_prefetch=1,
        grid=(ns,),
        in_specs=[pl.BlockSpec((1, N_IDX_HEADS, IDX_DIM), lambda b, pt: (b, 0, 0)),
                  pl.BlockSpec((1, N_IDX_HEADS, 1), lambda b, pt: (b, 0, 0)),
                  pl.BlockSpec((1, 1, IDX_DIM), lambda b, pt: (b, 0, 0))]
        + _page_specs(npages, layer, (PAGE_SIZE, IDX_DIM)),
        out_specs=pl.BlockSpec((1, 1, ncol), lambda b, pt: (b, 0, 0)),
    )
    return pl.pallas_call(
        functools.partial(_dsa_scores_kernel, npages=npages),
        grid_spec=gs,
        out_shape=jax.ShapeDtypeStruct((ns, 1, ncol), F32),
        compiler_params=_cparams(("parallel",)),
    )(page_table, iq, w_col, ikn, *([cik] * npages))


def _select_kernel(s_ref, o_ref, key_ref, eq_ref, *, n_vis, k_sel):
    vis = _iota(s_ref.shape, 1) < n_vis
    key_ref[...] = _sortable_key(jnp.where(vis, s_ref[...], NEG_INF))
    sel = _select_topk(key_ref, eq_ref, vis, k_sel)
    o_ref[...] = jnp.where(sel, 1.0, 0.0)


def select_rows(scores, n_vis, k_sel):
    return pl.pallas_call(
        functools.partial(_select_kernel, n_vis=n_vis, k_sel=k_sel),
        out_shape=jax.ShapeDtypeStruct(scores.shape, F32),
        scratch_shapes=[pltpu.VMEM(scores.shape, I32), pltpu.VMEM(scores.shape, F32)],
        compiler_params=pltpu.CompilerParams(vmem_limit_bytes=VMEM_LIMIT),
    )(scores)


def _dsa_sample_kernel(pt_ref, q_ref, kn_ref, vn_ref, mk_ref, *rest, npages):
    k_refs = rest[0:npages]
    v_refs = rest[npages:2 * npages]
    o_ref = rest[2 * npages]
    q8 = _head_rows(q_ref[0], HEAD_DIM ** -0.5)
    q8b = q8.astype(BF16)
    past = npages * PAGE_SIZE
    s_new = jnp.sum(q8 * kn_ref[0], axis=-1, keepdims=True)
    s_new = jnp.where(mk_ref[0, :, past:past + 1] > 0.0, s_new, NEG_INF)
    ss = []
    m = s_new
    for p in range(npages):
        s = _dot_nt(q8b, k_refs[p][0, 0].astype(BF16))
        s = jnp.where(mk_ref[0, :, p * PAGE_SIZE:(p + 1) * PAGE_SIZE] > 0.0, s, NEG_INF)
        ss.append(s)
        m = jnp.maximum(m, jnp.max(s, axis=-1, keepdims=True))
    pn = jnp.exp(s_new - m)
    l = pn
    acc = pn * vn_ref[0]
    for p in range(npages):
        e = jnp.exp(ss[p] - m)
        l = l + jnp.sum(e, axis=-1, keepdims=True)
        acc = acc + _dot(e.astype(BF16), v_refs[p][0, 0].astype(BF16))
    o_ref[0] = _merge_heads(acc / l)


def dsa_sample(page_table, q, kn, vn, mask, ck, cv, layer):
    ns, npages = page_table.shape
    ncol = mask.shape[-1]
    row = pl.BlockSpec((1, 1, HW), lambda b, pt: (b, 0, 0))
    gs = pltpu.PrefetchScalarGridSpec(
        num_scalar_prefetch=1,
        grid=(ns,),
        in_specs=[row, row, row, pl.BlockSpec((1, 1, ncol), lambda b, pt: (b, 0, 0))]
        + _page_specs(npages, layer, (PAGE_SIZE, HW))
        + _page_specs(npages, layer, (PAGE_SIZE, HW)),
        out_specs=row,
    )
    return pl.pallas_call(
        functools.partial(_dsa_sample_kernel, npages=npages),
        grid_spec=gs,
        out_shape=jax.ShapeDtypeStruct((ns, 1, HW), F32),
        compiler_params=_cparams(("parallel",)),
    )(page_table, q, kn, vn, mask, *([ck] * npages), *([cv] * npages))


def _outproj_kernel(x_ref, fo_ref, do_ref, ho_ref, w_ref, g_ref, b_ref, o_ref):
    sub = (_dot(fo_ref[...].astype(BF16), w_ref[0:HW, :])
           + _dot(do_ref[...].astype(BF16), w_ref[HW:2 * HW, :])
           + _dot(ho_ref[...].astype(BF16), w_ref[2 * HW:, :]))
    o_ref[...] = _layer_norm(ALPHA * x_ref[...] + sub, g_ref[...], b_ref[...])


def outproj_norm(x_all, fo, do, ho, w_out, g, b, tm):
    n = x_all.shape[0]
    rowspec = lambda w: pl.BlockSpec((tm, w), lambda i: (i, 0))
    const = lambda shp: pl.BlockSpec(shp, lambda i: (0, 0))
    return pl.pallas_call(
        _outproj_kernel,
        grid=(n // tm,),
        in_specs=[rowspec(D_MODEL), rowspec(HW), rowspec(HW), rowspec(HGW),
                  const((D_MODEL, D_MODEL)), const((1, D_MODEL)), const((1, D_MODEL))],
        out_specs=rowspec(D_MODEL),
        out_shape=jax.ShapeDtypeStruct((n, D_MODEL), F32),
        compiler_params=_cparams(("parallel",)),
    )(x_all, fo, do, ho, w_out, g, b)


def _moe_kernel(x_ref, wrh_ref, wrl_ref, br_ref, w1_ref, w3_ref, w2_ref, g_ref, b_ref, o_ref,
                xb_scr, gate_scr, acc_scr):
    e = pl.program_id(1)
    tm = x_ref.shape[0]
    lane = _iota((tm, 128), 1)

    @pl.when(e == 0)
    def _():
        x = x_ref[...]
        xh = x.astype(BF16)
        xl = (x - xh.astype(F32)).astype(BF16)
        logits = (_dot(xh, wrh_ref[...]) + _dot(xl, wrh_ref[...]) + _dot(xh, wrl_ref[...])
                  + br_ref[...])
        gl = jnp.where(lane < N_GROUPS, logits, NEG_INF)
        gmax = jnp.max(gl, axis=-1, keepdims=True)
        p_top = 1.0 / jnp.sum(jnp.exp(gl - gmax), axis=-1, keepdims=True)
        lanef = lane.astype(F32)
        grp = jnp.min(jnp.where(gl == gmax, lanef, 1024.0), axis=-1, keepdims=True)
        lo = N_GROUPS + EPG * grp
        ev = jnp.where((lanef >= lo) & (lanef < lo + EPG), logits, NEG_INF)
        v1 = jnp.max(ev, axis=-1, keepdims=True)
        i1 = jnp.min(jnp.where(ev == v1, lanef, 1024.0), axis=-1, keepdims=True)
        ev2 = jnp.where(lanef == i1, NEG_INF, ev)
        v2 = jnp.max(ev2, axis=-1, keepdims=True)
        i2 = jnp.min(jnp.where(ev2 == v2, lanef, 1024.0), axis=-1, keepdims=True)
        t = jnp.exp(v2 - v1)
        g1 = p_top / (1.0 + t)
        g2 = p_top * t / (1.0 + t)
        gate_scr[...] = jnp.where(lanef == i1, g1, 0.0) + jnp.where(lanef == i2, g2, 0.0)
        xb_scr[...] = xh
        acc_scr[...] = jnp.zeros(acc_scr.shape, F32)

    xb = xb_scr[...]
    a = _dot(xb, w1_ref[0])
    h = (a / (1.0 + jnp.exp(-a))) * _dot(xb, w3_ref[0])
    y = _dot(h.astype(BF16), w2_ref[0])
    ge = jnp.sum(jnp.where(lane == e + N_GROUPS, gate_scr[...], 0.0), axis=-1, keepdims=True)
    acc_scr[...] += y * ge

    @pl.when(e == N_EXPERTS - 1)
    def _():
        o_ref[...] = _layer_norm(ALPHA * x_ref[...] + acc_scr[...], g_ref[...], b_ref[...])


def moe_norm(x_all, wr_hi, wr_lo, br, w1, w3, w2, g, b, tm):
    n = x_all.shape[0]
    const = lambda shp: pl.BlockSpec(shp, lambda i, e: (0, 0))
    return pl.pallas_call(
        _moe_kernel,
        grid=(n // tm, N_EXPERTS),
        in_specs=[pl.BlockSpec((tm, D_MODEL), lambda i, e: (i, 0)),
                  const((D_MODEL, 128)), const((D_MODEL, 128)), const((1, 128)),
                  pl.BlockSpec((1, D_MODEL, D_FF), lambda i, e: (e, 0, 0)),
                  pl.BlockSpec((1, D_MODEL, D_FF), lambda i, e: (e, 0, 0)),
                  pl.BlockSpec((1, D_FF, D_MODEL), lambda i, e: (e, 0, 0)),
                  const((1, D_MODEL)), const((1, D_MODEL))],
        out_specs=pl.BlockSpec((tm, D_MODEL), lambda i, e: (i, 0)),
        out_shape=jax.ShapeDtypeStruct((n, D_MODEL), F32),
        scratch_shapes=[pltpu.VMEM((tm, D_MODEL), BF16), pltpu.VMEM((tm, 128), F32),
                        pltpu.VMEM((tm, D_MODEL), F32)],
        compiler_params=_cparams(("parallel", "arbitrary")),
    )(x_all, wr_hi, wr_lo, br, w1, w3, w2, g, b)


def router_weights(wg, bg, we, be):
    w = jnp.concatenate([wg, we, jnp.zeros((D_MODEL, 128 - N_GROUPS - N_EXPERTS), F32)], -1)
    hi = w.astype(BF16)
    lo = (w - hi.astype(F32)).astype(BF16)
    br = jnp.concatenate([bg, be, jnp.zeros((128 - N_GROUPS - N_EXPERTS,), F32)])[None, :]
    return hi, lo, br


def forward(x_prompt, x_sample, cache_fox_k, cache_fox_v, cache_fox_logf, cache_dsa_k, cache_dsa_v,
            cache_idx_k, state_hgrn, page_table, w_in, b_fox, hg_lb, hg_norm, w_out, ln1_g, ln1_b,
            moe_wg, moe_bg, moe_we, moe_be, moe_w1, moe_w3, moe_w2, ln2_g, ln2_b,
            *, tm, tq_fox, tq_dsa, chunk, tm_moe, bb_hg):
    nb, seq, _ = x_prompt.shape
    ns = x_sample.shape[0]
    depth = w_in.shape[0]
    npr = nb * seq
    n_rows = npr + tm
    npages = page_table.shape[1]
    pool = cache_fox_k.shape[1]

    x = jnp.concatenate([x_prompt.reshape(npr, D_MODEL), x_sample.reshape(ns, D_MODEL),
                         jnp.zeros((tm - ns, D_MODEL), F32)], 0)
    w_p = permute_w_in(w_in)
    t64, t32, ts = rope_tables(seq, tm)
    lb, llb, l1m, om = lb_params(hg_lb)
    ck_f = cache_fox_k.reshape(depth, pool, PAGE_SIZE, HW)
    cv_f = cache_fox_v.reshape(depth, pool, PAGE_SIZE, HW)
    ck_d = cache_dsa_k.reshape(depth, pool, PAGE_SIZE, HW)
    cv_d = cache_dsa_v.reshape(depth, pool, PAGE_SIZE, HW)
    clf_t = jnp.swapaxes(cache_fox_logf, 2, 3)
    w_out_b = w_out.astype(BF16)
    w1_b, w3_b, w2_b = moe_w1.astype(BF16), moe_w3.astype(BF16), moe_w2.astype(BF16)
    k_sel_s = min(TOPK_MAX, (npages * PAGE_SIZE + 1) // 4)

    outs_p = [[] for _ in range(7)]
    outs_s = [[] for _ in range(7)]
    for l in range(depth):
        bf_row = jnp.concatenate([b_fox[l], jnp.zeros((128 - H_FOX,), F32)])[None, :]
        p_all = project(x, w_p[l], t64, t32, ts, bf_row, tm, seq // tm, npr // tm)
        sm_p = p_all[:npr, C_SM:C_SM + 128]
        ps = p_all[npr:npr + ns]
        sm_s = ps[:, C_SM:C_SM + 128]

        lf_p = sm_p[:, 0:H_FOX].reshape(nb, seq, H_FOX)
        lf_t = jnp.concatenate([jnp.swapaxes(lf_p, 1, 2), jnp.zeros((nb, 8 - H_FOX, seq), F32)], 1)
        fcum = cumsum_rows(lf_t.reshape(nb * 8, seq)).reshape(nb, 8, seq // tq_fox, tq_fox)
        fcum = jnp.transpose(fcum, (0, 2, 1, 3))
        fo_p = fox_prompt(p_all, fcum, nb, seq, tq_fox)
        do_p = dsa_prompt(p_all, nb, seq, tq_dsa)
        par = jnp.concatenate([llb[l:l + 1], l1m[l:l + 1], om[l:l + 1], hg_norm[l][None, :],
                               jnp.zeros((4, HGW), F32)], 0)
        ho_p, st_p = hgrn_prompt(p_all, par, nb, seq, chunk)

        r3 = lambda a: a.reshape(ns, 1, a.shape[-1])
        lfn = jnp.concatenate([sm_s[:, 0:H_FOX], jnp.zeros((ns, 8 - H_FOX), F32)], -1)
        lfn_col = jnp.tile(lfn, (1, npages))[:, :, None]
        fo_s = fox_sample(page_table, r3(ps[:, C_FQ:C_FQ + HW]), r3(ps[:, C_FK:C_FK + HW]),
                          r3(ps[:, C_FV:C_FV + HW]), lfn_col, ck_f, cv_f, clf_t, l)
        iq_s = ps[:, C_IQ:C_IQ + HW].reshape(ns, N_IDX_HEADS, IDX_DIM)
        w_col = sm_s[:, SM_IW:SM_IW + N_IDX_HEADS][:, :, None]
        ik_s = sm_s[:, SM_IK:SM_IK + IDX_DIM]
        sc = dsa_sample_scores(page_table, iq_s, w_col, r3(ik_s), cache_idx_k, l)
        mask = select_rows(sc.reshape(ns, -1), npages * PAGE_SIZE + 1, k_sel_s)
        do_s = dsa_sample(page_table, r3(ps[:, C_DQ:C_DQ + HW]), r3(ps[:, C_DK:C_DK + HW]),
                          r3(ps[:, C_DV:C_DV + HW]), mask[:, None, :], ck_d, cv_d, l)
        colv = lambda a: a.reshape(ns, H_HG, HG_D, 1)
        rowv = lambda a: a.reshape(ns, H_HG, 1, HG_D)
        par_col = jnp.stack([llb[l], l1m[l], om[l]]).reshape(3, H_HG, HG_D, 1)
        ho_s, st_s = hgrn_sample(colv(ps[:, C_HQ:C_HQ + HGW]), colv(ps[:, C_HF:C_HF + HGW]),
                                 rowv(ps[:, C_HI:C_HI + HGW]), rowv(ps[:, C_HG:C_HG + HGW]),
                                 state_hgrn, l, par_col, hg_norm[l].reshape(H_HG, 1, HG_D), bb_hg)

        padr = lambda a: jnp.zeros((tm - ns, a.shape[-1]), F32)
        cat = lambda a, s: jnp.concatenate([a, s, padr(a)], 0)
        fo = cat(fo_p, fo_s.reshape(ns, HW))
        do = cat(do_p, do_s.reshape(ns, HW))
        ho = cat(ho_p, ho_s.reshape(ns, HGW))
        x = outproj_norm(x, fo, do, ho, w_out_b[l], ln1_g[l][None, :], ln1_b[l][None, :], tm)
        wr_hi, wr_lo, br = router_weights(moe_wg[l], moe_bg[l], moe_we[l], moe_be[l])
        x = moe_norm(x, wr_hi, wr_lo, br, w1_b[l], w3_b[l], w2_b[l],
                     ln2_g[l][None, :], ln2_b[l][None, :], tm_moe)

        pp = p_all[:npr]
        hp = lambda a: a.reshape(nb, seq, 4, HEAD_DIM)
        for lst, a in zip(outs_p, (hp(pp[:, C_FK:C_FK + HW]), hp(pp[:, C_FV:C_FV + HW]), lf_p,
                                   hp(pp[:, C_DK:C_DK + HW]), hp(pp[:, C_DV:C_DV + HW]),
                                   sm_p[:, SM_IK:SM_IK + IDX_DIM].reshape(nb, seq, IDX_DIM), st_p)):
            lst.append(a)
        hs = lambda a: a.reshape(ns, 1, 4, HEAD_DIM)
        for lst, a in zip(outs_s, (hs(ps[:, C_FK:C_FK + HW]), hs(ps[:, C_FV:C_FV + HW]),
                                   sm_s[:, 0:H_FOX].reshape(ns, 1, H_FOX),
                                   hs(ps[:, C_DK:C_DK + HW]), hs(ps[:, C_DV:C_DV + HW]),
                                   ik_s.reshape(ns, 1, IDX_DIM), st_s)):
            lst.append(a)

    y_p = x[:npr].reshape(nb, seq, D_MODEL)
    y_s = x[npr:npr + ns].reshape(ns, 1, D_MODEL)
    return (y_p, y_s) + tuple(jnp.stack(a) for a in outs_p) + tuple(jnp.stack(a) for a in outs_s)


def kernel(x_prompt, x_sample, cache_fox_k, cache_fox_v, cache_fox_logf, cache_dsa_k, cache_dsa_v, cache_idx_k, state_hgrn, page_table, w_in, b_fox, hg_lb, hg_norm, w_out, ln1_g, ln1_b, moe_wg, moe_bg, moe_we, moe_be, moe_w1, moe_w3, moe_w2, ln2_g, ln2_b):
    return forward(x_prompt, x_sample, cache_fox_k, cache_fox_v, cache_fox_logf, cache_dsa_k,
                   cache_dsa_v, cache_idx_k, state_hgrn, page_table, w_in, b_fox, hg_lb, hg_norm,
                   w_out, ln1_g, ln1_b, moe_wg, moe_bg, moe_we, moe_be, moe_w1, moe_w3, moe_w2,
                   ln2_g, ln2_b, tm=256, tq_fox=256, tq_dsa=256, chunk=128, tm_moe=1280, bb_hg=8)
```

```python
import functools

import numpy as np
import jax
import jax.numpy as jnp
from jax import lax
from jax.experimental import pallas as pl
from jax.experimental.pallas import tpu as pltpu

F32 = jnp.float32
BF16 = jnp.bfloat16
I32 = jnp.int32

D_MODEL = 1024
DEPTH = 4
HEAD_DIM = 64
H_FOX = 4
H_DSA = 4
H_HG = 8
HG_D = 64
N_IDX_HEADS = 8
IDX_DIM = 32
PAGE_SIZE = 128
TOPK_MAX = 256
ROPE_THETA = 500000.0
N_GROUPS = 4
EPG = 4
N_EXPERTS = 16
D_FF = 512
LN_EPS = 1e-5
ALPHA = (2 * DEPTH) ** 0.25
HW = 256
HGW = 512

C_FQ, C_FK, C_FV, C_DQ, C_DK, C_DV, C_IQ, C_IKT = (i * 256 for i in range(8))
C_HQ, C_HF, C_HI, C_HG = 2048, 2560, 3072, 3584
C_SM = 4096
N_P = 4224
SM_IK = 32
SM_IW = 64

VMEM_LIMIT = 56 * 1024 * 1024
NEG_INF = float("-inf")


def _cparams(sem):
    return pltpu.CompilerParams(dimension_semantics=sem, vmem_limit_bytes=VMEM_LIMIT)


def _iota(shape, dim):
    return lax.broadcasted_iota(I32, shape, dim)


def _dot(a, b):
    return jnp.dot(a, b, preferred_element_type=F32)


def _dot_nt(a, b):
    return lax.dot_general(a, b, (((1,), (1,)), ((), ())), preferred_element_type=F32)


def _split3(x):
    h = x.astype(BF16)
    r = x - h.astype(F32)
    m = r.astype(BF16)
    l = (r - m.astype(F32)).astype(BF16)
    return h, m, l


def _dot3_l(x, w):
    h, m, l = _split3(x)
    return _dot(h, w) + _dot(m, w) + _dot(l, w)


def _dot3_r(w, x):
    h, m, l = _split3(x)
    return _dot(w, h) + _dot(w, m) + _dot(w, l)


def _layer_norm(y, g, b):
    mu = jnp.mean(y, axis=-1, keepdims=True)
    yc = y - mu
    var = jnp.mean(yc * yc, axis=-1, keepdims=True)
    return yc * lax.rsqrt(var + LN_EPS) * g + b


def _lb_kernel(x_ref, lb_ref, llb_ref, l1m_ref, om_ref):
    x = x_ref[...]
    m = jnp.max(x, axis=0, keepdims=True)
    e = jnp.exp(x - m)
    p = e / jnp.sum(e, axis=0, keepdims=True)
    n = x.shape[0]
    c = p[0:1]
    c0 = c
    for l in range(n):
        if l > 0:
            c = c + p[l:l + 1]
        lb = c - c0
        lb_ref[l:l + 1, :] = lb
        llb_ref[l:l + 1, :] = jnp.log(lb)
        l1m_ref[l:l + 1, :] = jnp.log1p(-lb)
        om_ref[l:l + 1, :] = 1.0 - lb


def lb_params(hg_lb):
    shp = jax.ShapeDtypeStruct(hg_lb.shape, F32)
    return pl.pallas_call(_lb_kernel, out_shape=(shp, shp, shp, shp))(hg_lb.astype(F32))


def _rope(v, t_ref, half):
    n = v.shape[-1]
    return (v * t_ref[0] + pltpu.roll(v, n - half, 1) * t_ref[1]
            + pltpu.roll(v, half, 1) * t_ref[2])


def _proj_kernel(x_ref, w_ref, t64_ref, t32_ref, ts_ref, bf_ref, o_ref):
    x = x_ref[...].astype(BF16)

    def seg(a, wd):
        return _dot(x, w_ref[:, a:a + wd])

    o_ref[:, C_FQ:C_FQ + 768] = seg(C_FQ, 768)
    o_ref[:, C_DQ:C_DQ + HW] = _rope(seg(C_DQ, HW), t64_ref, 8)
    o_ref[:, C_DK:C_DK + HW] = _rope(seg(C_DK, HW), t64_ref, 8)
    o_ref[:, C_DV:C_DV + HW] = seg(C_DV, HW)
    o_ref[:, C_IQ:C_IQ + HW] = _rope(seg(C_IQ, HW), t32_ref, 4)
    o_ref[:, C_IKT:C_IKT + HW] = _rope(seg(C_IKT, HW), t32_ref, 4)
    o_ref[:, C_HQ:C_HQ + 1024] = seg(C_HQ, 1024)
    o_ref[:, C_HI:C_HI + 1024] = seg(C_HI, 1024)
    sm = _rope(seg(C_SM, 128), ts_ref, 4)
    z = sm + bf_ref[...]
    logsig = jnp.minimum(z, 0.0) - jnp.log1p(jnp.exp(-jnp.abs(z)))
    o_ref[:, C_SM:C_SM + 128] = jnp.where(_iota(sm.shape, 1) < H_FOX, logsig, sm)


def project(x_all, w_p, t64, t32, ts, bfox_row, tm, tiles_per_seq, n_seq_tiles):
    n = x_all.shape[0]
    nt = n // tm

    def tab_map(i):
        return (0, jnp.where(i < n_seq_tiles, i % tiles_per_seq, tiles_per_seq), 0)

    return pl.pallas_call(
        _proj_kernel,
        grid=(nt,),
        in_specs=[
            pl.BlockSpec((tm, D_MODEL), lambda i: (i, 0)),
            pl.BlockSpec((D_MODEL, N_P), lambda i: (0, 0)),
            pl.BlockSpec((3, tm, HW), tab_map),
            pl.BlockSpec((3, tm, HW), tab_map),
            pl.BlockSpec((3, tm, 128), tab_map),
            pl.BlockSpec((1, 128), lambda i: (0, 0)),
        ],
        out_specs=pl.BlockSpec((tm, N_P), lambda i: (i, 0)),
        out_shape=jax.ShapeDtypeStruct((n, N_P), F32),
        compiler_params=_cparams(("parallel",)),
    )(x_all, w_p, t64, t32, ts, bfox_row)


def rope_tables(seq, tm):
    pos = jnp.concatenate([jnp.arange(seq), jnp.full((tm,), seq)]).astype(F32)

    def head_tabs(hd):
        rot = hd // 4
        half = rot // 2
        inv = ROPE_THETA ** (-jnp.arange(half, dtype=F32) * 2.0 / rot)
        ang = pos[:, None] * inv[None, :]
        cos, sin = jnp.cos(ang), jnp.sin(ang)
        n = pos.shape[0]
        one = jnp.ones((n, hd - rot), F32)
        zero = jnp.zeros((n, hd - rot), F32)
        zh = jnp.zeros((n, half), F32)
        c = jnp.concatenate([cos, cos, one], -1)
        sa = jnp.concatenate([-sin, zh, zero], -1)
        sb = jnp.concatenate([zh, sin, zero], -1)
        return c, sa, sb

    t64 = jnp.stack([jnp.tile(a, (1, HW // 64)) for a in head_tabs(64)])
    h32 = head_tabs(32)
    t32 = jnp.stack([jnp.tile(a, (1, HW // 32)) for a in h32])
    n = pos.shape[0]
    fill = [jnp.ones, jnp.zeros, jnp.zeros]
    ts = jnp.stack([jnp.concatenate([f((n, SM_IK), F32), a, f((n, 128 - SM_IK - 32), F32)], -1)
                    for f, a in zip(fill, h32)])
    return t64, t32, ts


def permute_w_in(w_in):
    L = w_in.shape[0]
    z = lambda n: jnp.zeros((L, D_MODEL, n), w_in.dtype)
    ik = w_in[:, :, 1796:1828]
    small = jnp.concatenate([w_in[:, :, 768:772], z(SM_IK - 4), ik, w_in[:, :, 1828:1836],
                             z(128 - SM_IW - 8)], -1)
    w = jnp.concatenate([w_in[:, :, 0:768], w_in[:, :, 772:1540], w_in[:, :, 1540:1796],
                         jnp.tile(ik, (1, 1, 8)), w_in[:, :, 1836:3884], small], -1)
    return w.astype(BF16)


def _cumsum_kernel(x_ref, tri_ref, o_ref):
    r, n = x_ref.shape
    carry = jnp.zeros((r, 1), F32)
    tri = tri_ref[...]
    for c in range(n // 128):
        blk = x_ref[:, c * 128:(c + 1) * 128]
        cs = _dot3_l(blk, tri) + carry
        o_ref[:, c * 128:(c + 1) * 128] = cs
        carry = carry + jnp.sum(blk, axis=-1, keepdims=True)


def cumsum_rows(x):
    tri = jnp.asarray(np.triu(np.ones((128, 128), np.float32)), BF16)
    return pl.pallas_call(_cumsum_kernel, out_shape=jax.ShapeDtypeStruct(x.shape, F32))(x, tri)


def _fox_prompt_kernel(q_ref, k_ref, v_ref, f_ref, o_ref, qm_ref, acc_ref, *, tq):
    qi = pl.program_id(1)
    q = q_ref[...] * (HEAD_DIM ** -0.5)
    hl = _iota(q.shape, 1) >> 6
    for h in range(H_FOX):
        qm_ref[h] = jnp.where(hl == h, q, 0.0).astype(BF16)
    acc_ref[...] = jnp.zeros(acc_ref.shape, F32)

    def step(j, carry, diag):
        ms, ls = carry
        start = pl.multiple_of(j * tq, tq)
        kb = k_ref[pl.ds(start, tq), :].astype(BF16)
        vb = v_ref[pl.ds(start, tq), :].astype(BF16)
        new_m, new_l = [], []
        for h in range(H_FOX):
            s = _dot_nt(qm_ref[h], kb) - f_ref[0, j][h:h + 1, :]
            if diag:
                s = jnp.where(_iota(s.shape, 1) <= _iota(s.shape, 0), s, NEG_INF)
            m = jnp.maximum(ms[h], jnp.max(s, axis=-1, keepdims=True))
            p = jnp.exp(s - m)
            a = jnp.exp(ms[h] - m)
            new_l.append(a * ls[h] + jnp.sum(p, axis=-1, keepdims=True))
            acc_ref[h] = a * acc_ref[h] + _dot(p.astype(BF16), vb)
            new_m.append(m)
        return tuple(new_m), tuple(new_l)

    init = (tuple(jnp.full((tq, 1), NEG_INF, F32) for _ in range(H_FOX)),
            tuple(jnp.zeros((tq, 1), F32) for _ in range(H_FOX)))
    carry = lax.fori_loop(0, qi, lambda j, c: step(j, c, False), init)
    ms, ls = step(qi, carry, True)
    out = jnp.zeros((tq, HW), F32)
    for h in range(H_FOX):
        out = out + jnp.where(hl == h, acc_ref[h] / ls[h], 0.0)
    o_ref[...] = out


def fox_prompt(p_all, fcum, nb, seq, tq):
    nq = seq // tq
    return pl.pallas_call(
        functools.partial(_fox_prompt_kernel, tq=tq),
        grid=(nb, nq),
        in_specs=[
            pl.BlockSpec((tq, HW), lambda b, i: (b * nq + i, C_FQ // HW)),
            pl.BlockSpec((seq, HW), lambda b, i: (b, C_FK // HW)),
            pl.BlockSpec((seq, HW), lambda b, i: (b, C_FV // HW)),
            pl.BlockSpec((1, nq, 8, tq), lambda b, i: (b, 0, 0, 0)),
        ],
        out_specs=pl.BlockSpec((tq, HW), lambda b, i: (b * nq + i, 0)),
        out_shape=jax.ShapeDtypeStruct((nb * seq, HW), F32),
        scratch_shapes=[pltpu.VMEM((H_FOX, tq, HW), BF16), pltpu.VMEM((H_FOX, tq, HW), F32)],
        compiler_params=_cparams(("parallel", "parallel")),
    )(p_all, p_all, p_all, fcum)


def _sortable_key(score):
    bits = lax.bitcast_convert_type(score + 0.0, I32)
    return jnp.where(bits < 0, bits ^ jnp.int32(0x7FFFFFFF), bits)


def _select_topk(key_ref, eq_ref, vis, k):
    rows, cols = key_ref.shape
    kf = float(k)

    def body(i, thr):
        cand = thr + jnp.left_shift(jnp.int32(1), 31 - i)
        cnt = jnp.sum(jnp.where(key_ref[...] >= cand, 1.0, 0.0), axis=-1, keepdims=True)
        return jnp.where(cnt >= kf, cand, thr)

    thr = lax.fori_loop(0, 32, body, jnp.full((rows, 1), -2 ** 31, I32))
    key = key_ref[...]
    gt = key > thr
    eqf = jnp.where(key == thr, 1.0, 0.0)
    eq_ref[...] = eqf
    need = kf - jnp.sum(jnp.where(gt, 1.0, 0.0), axis=-1, keepdims=True)
    n_eq = jnp.sum(eqf, axis=-1, keepdims=True)
    nbits = int(cols).bit_length()
    tied = (n_eq > need) & (thr > _sortable_key(jnp.full((1, 1), NEG_INF, F32)))
    any_tied = jnp.max(jnp.where(tied, 1.0, 0.0)) > 0.0

    def search():
        def body2(i, x):
            cand = x + jnp.left_shift(jnp.int32(1), nbits - 1 - i)
            col = _iota((rows, cols), 1)
            c = jnp.sum(jnp.where(col < cand, eq_ref[...], 0.0), axis=-1, keepdims=True)
            return jnp.where(c < need, cand, x)

        return lax.fori_loop(0, nbits, body2, jnp.zeros((rows, 1), I32))

    x = lax.cond(any_tied, search, lambda: jnp.full((rows, 1), cols, I32))
    x = jnp.where(tied, x, cols)
    col = _iota((rows, cols), 1)
    return vis & (gt | ((eq_ref[...] > 0.0) & (col <= x)))


def _dsa_prompt_kernel(dq_ref, dk_ref, dv_ref, iq_ref, ikt_ref, sm_ref, o_ref, key_ref, eq_ref,
                       *, tq, k_sel, qi):
    nvis = (qi + 1) * tq
    iq = iq_ref[...] * (IDX_DIM ** -0.5)
    ikt = ikt_ref[0:nvis, :].astype(BF16)
    sm = sm_ref[...]
    il = _iota(iq.shape, 1) >> 5
    score = jnp.zeros((tq, nvis), F32)
    for h in range(N_IDX_HEADS):
        rel = jnp.maximum(_dot_nt(jnp.where(il == h, iq, 0.0).astype(BF16), ikt), 0.0)
        w = sm[:, SM_IW + h:SM_IW + h + 1] * (N_IDX_HEADS ** -0.5)
        score = score + w * rel
    vis = _iota((tq, nvis), 1) <= _iota((tq, nvis), 0) + qi * tq
    key_ref[...] = _sortable_key(jnp.where(vis, score, NEG_INF))
    sel = _select_topk(key_ref, eq_ref, vis, k_sel)

    dq = dq_ref[...] * (HEAD_DIM ** -0.5)
    dk = dk_ref[0:nvis, :].astype(BF16)
    dv = dv_ref[0:nvis, :].astype(BF16)
    hl = _iota(dq.shape, 1) >> 6
    out = jnp.zeros((tq, HW), F32)
    for h in range(H_DSA):
        s = _dot_nt(jnp.where(hl == h, dq, 0.0).astype(BF16), dk)
        s = jnp.where(sel, s, NEG_INF)
        m = jnp.max(s, axis=-1, keepdims=True)
        p = jnp.exp(s - m)
        l = jnp.sum(p, axis=-1, keepdims=True)
        out = out + jnp.where(hl == h, _dot(p.astype(BF16), dv) / l, 0.0)
    o_ref[0] = out


def dsa_prompt(p_all, nb, seq, tq):
    nq = seq // tq
    k_sel = min(TOPK_MAX, seq // 4)
    outs = []
    for qi in range(nq):
        nvis = (qi + 1) * tq
        rows = nvis if seq % nvis == 0 else seq
        kspec = lambda c, rows=rows: pl.BlockSpec((rows, HW), lambda b: (b * (seq // rows), c))
        qspec = lambda c, w, qi=qi: pl.BlockSpec((tq, w), lambda b: (b * nq + qi, c))
        outs.append(pl.pallas_call(
            functools.partial(_dsa_prompt_kernel, tq=tq, k_sel=k_sel, qi=qi),
            grid=(nb,),
            in_specs=[qspec(C_DQ // HW, HW), kspec(C_DK // HW), kspec(C_DV // HW),
                      qspec(C_IQ // HW, HW), kspec(C_IKT // HW), qspec(C_SM // 128, 128)],
            out_specs=pl.BlockSpec((1, tq, HW), lambda b: (b, 0, 0)),
            out_shape=jax.ShapeDtypeStruct((nb, tq, HW), F32),
            scratch_shapes=[pltpu.VMEM((tq, nvis), I32), pltpu.VMEM((tq, nvis), F32)],
            compiler_params=_cparams(("parallel",)),
        )(p_all, p_all, p_all, p_all, p_all, p_all))
    return jnp.stack(outs, axis=1).reshape(nb * seq, HW)


def hgrn_level_mats(c):
    nl = int(np.log2(c))
    t = np.arange(c)
    mats = [(t[None, :] <= t[:, None]).astype(np.float32)]
    masks = []
    for L in range(nl):
        bit = (t >> L) & 1
        lo = (t >> L) << L
        w = np.zeros((c, c), np.float32)
        for r in range(c):
            if bit[r]:
                w[r, lo[r]:r + 1] = 1.0
            else:
                w[r, r + 1:lo[r] + (1 << L)] = 1.0
        mats.append(w)
        same = (t[:, None] >> (L + 1)) == (t[None, :] >> (L + 1))
        masks.append((bit[:, None] == 1) & (bit[None, :] == 0) & same)
    return np.stack(mats), np.stack(masks).astype(np.float32)


def _hgrn_gates(fl, llb, l1m, om):
    ls = jnp.minimum(fl, 0.0) - jnp.log1p(jnp.exp(-jnp.abs(fl)))
    b = l1m + ls
    logf = jnp.maximum(llb, b) + jnp.log1p(jnp.exp(-jnp.abs(llb - b)))
    kk = om / (1.0 + jnp.exp(fl))
    return logf, kk


def _hgrn_prompt_kernel(q_ref, f_ref, v_ref, g_ref, par_ref, wl_ref, ml_ref, bd_ref, o_ref, st_ref,
                        s_scr, *, c, nl):
    ci = pl.program_id(1)

    @pl.when(ci == 0)
    def _():
        s_scr[...] = jnp.zeros(s_scr.shape, F32)

    q = q_ref[...]
    v = v_ref[...]
    logf, kk = _hgrn_gates(f_ref[...], par_ref[0:1, :], par_ref[1:2, :], par_ref[2:3, :])
    lf3 = jnp.concatenate(_split3(logf), axis=-1)

    e_all = _dot(wl_ref[...], lf3)

    def rowsum(i):
        e = e_all[i * c:(i + 1) * c]
        return e[:, 0:HGW] + e[:, HGW:2 * HGW] + e[:, 2 * HGW:3 * HGW]

    bd = bd_ref[...]
    vb = v.astype(BF16)
    row = _iota((c, HGW), 0)
    hl = _iota((c, HW), 1) >> 6
    a_heads = [jnp.zeros((c, c), F32) for _ in range(H_HG)]
    for L in range(nl):
        e = rowsum(1 + L)
        x = (jnp.where(((row >> L) & 1) == 1, q, kk) * jnp.exp(e)).astype(BF16)
        msk = ml_ref[L] > 0.0
        for h in range(H_HG):
            xh = x[:, (h // 4) * HW:(h // 4 + 1) * HW]
            lhs = jnp.where(hl == (h % 4), xh, jnp.zeros_like(xh))
            a_heads[h] = a_heads[h] + jnp.where(msk, _dot_nt(lhs, xh), 0.0)
    o = _dot((q * kk).astype(BF16), bd) * v
    intra = []
    for half in range(2):
        acc = jnp.zeros((c, HW), F32)
        vh = vb[:, half * HW:(half + 1) * HW]
        for hh in range(4):
            acc = acc + jnp.where(hl == hh, _dot(a_heads[half * 4 + hh].astype(BF16), vh), 0.0)
        intra.append(acc)
    o = o + jnp.concatenate(intra, axis=-1)
    g = rowsum(0)
    st = s_scr[...]
    o = o + _dot_nt((q * jnp.exp(g)).astype(BF16), st.astype(BF16))
    gl = g[c - 1:c, :]
    kd = (kk * jnp.exp(gl - g)).astype(BF16)
    upd = _dot(v.T.astype(BF16), kd)
    s_new = st * jnp.exp(gl) + jnp.where(bd > 0, upd, 0.0)
    s_scr[...] = s_new
    st_ref[0] = s_new

    o2 = o * o
    o2h = o2.astype(BF16)
    o2l = (o2 - o2h.astype(F32)).astype(BF16)
    ms = (_dot(o2h, bd) + _dot(o2l, bd)) * (1.0 / HG_D)
    hg = g_ref[...]
    o_ref[...] = o * lax.rsqrt(ms + LN_EPS) * par_ref[3:4, :] * (hg / (1.0 + jnp.exp(-hg)))


def hgrn_prompt(p_all, par, nb, seq, c):
    nc = seq // c
    nl = int(np.log2(c))
    wl, ml = hgrn_level_mats(c)
    bd = np.kron(np.eye(H_HG, dtype=np.float32), np.ones((HG_D, HG_D), np.float32))
    o, st = pl.pallas_call(
        functools.partial(_hgrn_prompt_kernel, c=c, nl=nl),
        grid=(nb, nc),
        in_specs=[
            pl.BlockSpec((c, HGW), lambda b, i: (b * nc + i, C_HQ // HGW)),
            pl.BlockSpec((c, HGW), lambda b, i: (b * nc + i, C_HF // HGW)),
            pl.BlockSpec((c, HGW), lambda b, i: (b * nc + i, C_HI // HGW)),
            pl.BlockSpec((c, HGW), lambda b, i: (b * nc + i, C_HG // HGW)),
            pl.BlockSpec((8, HGW), lambda b, i: (0, 0)),
            pl.BlockSpec(((nl + 1) * c, c), lambda b, i: (0, 0)),
            pl.BlockSpec((nl, c, c), lambda b, i: (0, 0, 0)),
            pl.BlockSpec((HGW, HGW), lambda b, i: (0, 0)),
        ],
        out_specs=[
            pl.BlockSpec((c, HGW), lambda b, i: (b * nc + i, 0)),
            pl.BlockSpec((1, HGW, HGW), lambda b, i: (b, 0, 0)),
        ],
        out_shape=[jax.ShapeDtypeStruct((nb * seq, HGW), F32),
                   jax.ShapeDtypeStruct((nb, HGW, HGW), F32)],
        scratch_shapes=[pltpu.VMEM((HGW, HGW), F32)],
        compiler_params=_cparams(("parallel", "arbitrary")),
    )(p_all, p_all, p_all, p_all, par, jnp.asarray(wl.reshape(-1, c), BF16), jnp.asarray(ml, F32),
      jnp.asarray(bd, BF16))
    ar = jnp.arange(H_HG)
    st = st.reshape(nb, H_HG, HG_D, H_HG, HG_D)[:, ar, :, ar, :]
    return o, jnp.transpose(st, (1, 0, 3, 2))


def _hgrn_sample_kernel(q_ref, f_ref, v_ref, g_ref, s_ref, pc_ref, n_ref, o_ref, so_ref,
                        f_scr, k_scr):
    logf, kk = _hgrn_gates(f_ref[0], pc_ref[0, 0], pc_ref[1, 0], pc_ref[2, 0])
    f_scr[...] = jnp.exp(logf)
    k_scr[...] = kk
    v = v_ref[0]

    def body(d, o):
        sn = f_scr[pl.ds(d, 1), :] * s_ref[0, 0, d] + k_scr[pl.ds(d, 1), :] * v
        so_ref[0, d] = sn
        return o + q_ref[0, pl.ds(d, 1), :] * sn

    o = lax.fori_loop(0, HG_D, body, jnp.zeros(v.shape, F32), unroll=8)
    ms = jnp.mean(o * o, axis=0, keepdims=True)
    hg = g_ref[0]
    o_ref[0] = o * lax.rsqrt(ms + LN_EPS) * n_ref[0] * (hg / (1.0 + jnp.exp(-hg)))


def hgrn_sample(qt, ft, vt, gt, state_t, layer, par_col, norm_col):
    ns = qt.shape[-1]
    vec = pl.BlockSpec((1, HG_D, ns), lambda h: (h, 0, 0))
    return pl.pallas_call(
        _hgrn_sample_kernel,
        grid=(H_HG,),
        in_specs=[vec, vec, vec, vec,
                  pl.BlockSpec((1, 1, HG_D, HG_D, ns), lambda h: (layer, h, 0, 0, 0)),
                  pl.BlockSpec((3, 1, HG_D, 1), lambda h: (0, h, 0, 0)),
                  pl.BlockSpec((1, HG_D, 1), lambda h: (h, 0, 0))],
        out_specs=[vec, pl.BlockSpec((1, HG_D, HG_D, ns), lambda h: (h, 0, 0, 0))],
        out_shape=[jax.ShapeDtypeStruct((H_HG, HG_D, ns), F32),
                   jax.ShapeDtypeStruct((H_HG, HG_D, HG_D, ns), F32)],
        scratch_shapes=[pltpu.VMEM((HG_D, ns), F32), pltpu.VMEM((HG_D, ns), F32)],
        compiler_params=_cparams(("parallel",)),
    )(qt, ft, vt, gt, state_t, par_col, norm_col)


def _head_rows(q, scale):
    qb = jnp.broadcast_to(q * scale, (8, HW))
    return jnp.where((_iota((8, HW), 1) >> 6) == _iota((8, HW), 0), qb, 0.0)


def _merge_heads(acc):
    return jnp.sum(jnp.where((_iota((8, HW), 1) >> 6) == _iota((8, HW), 0), acc, 0.0),
                   axis=0, keepdims=True)


def _fox_sample_kernel(pt_ref, q_ref, kn_ref, vn_ref, lfn_ref, u_ref, mc_ref, *rest, npages):
    k_refs = rest[0:npages]
    v_refs = rest[npages:2 * npages]
    lf_refs = rest[2 * npages:3 * npages]
    o_ref = rest[3 * npages]
    lf_scr = rest[3 * npages + 1]
    q8 = _head_rows(q_ref[0], HEAD_DIM ** -0.5)
    q8b = q8.astype(BF16)
    lf_scr[...] = jnp.zeros(lf_scr.shape, F32)
    for p in range(npages):
        lf_scr[8 * p:8 * p + H_FOX, :] = lf_refs[p][0, 0]
    lf = lf_scr[...]
    tot = jnp.broadcast_to(jnp.sum(lf, axis=-1, keepdims=True), lf.shape)
    bias = _dot3_l(lf, u_ref[...]) + _dot3_r(mc_ref[...], tot) + lfn_ref[0]
    s_new = jnp.sum(q8 * kn_ref[0], axis=-1, keepdims=True)
    ss = []
    m = s_new
    for p in range(npages):
        s = _dot(q8b, k_refs[p][0, 0].astype(BF16)) + bias[8 * p:8 * p + 8, :]
        ss.append(s)
        m = jnp.maximum(m, jnp.max(s, axis=-1, keepdims=True))
    pn = jnp.exp(s_new - m)
    l = pn
    acc = pn * vn_ref[0]
    for p in range(npages):
        e = jnp.exp(ss[p] - m)
        l = l + jnp.sum(e, axis=-1, keepdims=True)
        acc = acc + _dot_nt(e.astype(BF16), v_refs[p][0, 0].astype(BF16))
    o_ref[0] = _merge_heads(acc / l)


def _page_specs(npages, layer, blk):
    def mk(p):
        return pl.BlockSpec((1, 1) + blk, lambda b, pt: (layer, pt[b, p], 0, 0))
    return [mk(p) for p in range(npages)]


def fox_sample(page_table, q, kn, vn, lfn_col, ck, cv, clf_t, layer):
    ns, npages = page_table.shape
    r = 8 * npages
    u = np.tril(np.ones((PAGE_SIZE, PAGE_SIZE), np.float32), -1)
    pg = np.arange(r) // 8
    hd = np.arange(r) % 8
    mc = ((hd[:, None] == hd[None, :]) & (pg[None, :] > pg[:, None])).astype(np.float32)
    row = pl.BlockSpec((1, 1, HW), lambda b, pt: (b, 0, 0))
    gs = pltpu.PrefetchScalarGridSpec(
        num_scalar_prefetch=1,
        grid=(ns,),
        in_specs=[row, row, row,
                  pl.BlockSpec((1, r, 1), lambda b, pt: (b, 0, 0)),
                  pl.BlockSpec((PAGE_SIZE, PAGE_SIZE), lambda b, pt: (0, 0)),
                  pl.BlockSpec((r, r), lambda b, pt: (0, 0))]
        + _page_specs(npages, layer, (HW, PAGE_SIZE))
        + _page_specs(npages, layer, (HW, PAGE_SIZE))
        + _page_specs(npages, layer, (H_FOX, PAGE_SIZE)),
        out_specs=row,
        scratch_shapes=[pltpu.VMEM((r, PAGE_SIZE), F32)],
    )
    return pl.pallas_call(
        functools.partial(_fox_sample_kernel, npages=npages),
        grid_spec=gs,
        out_shape=jax.ShapeDtypeStruct((ns, 1, HW), F32),
        compiler_params=_cparams(("parallel",)),
    )(page_table, q, kn, vn, lfn_col, jnp.asarray(u, BF16), jnp.asarray(mc, BF16),
      *([ck] * npages), *([cv] * npages), *([clf_t] * npages))


def _dsa_scores_kernel(pt_ref, iq_ref, w_ref, ikn_ref, *rest, npages):
    ik_refs = rest[0:npages]
    o_ref = rest[npages]
    iq = iq_ref[0] * (IDX_DIM ** -0.5)
    w = w_ref[0] * (N_IDX_HEADS ** -0.5)
    iqb = iq.astype(BF16)
    for p in range(npages):
        rel = jnp.maximum(_dot(iqb, ik_refs[p][0, 0].astype(BF16)), 0.0)
        o_ref[0, :, p * PAGE_SIZE:(p + 1) * PAGE_SIZE] = jnp.sum(w * rel, axis=0, keepdims=True)
    rel_n = jnp.maximum(jnp.sum(iq * ikn_ref[0], axis=-1, keepdims=True), 0.0)
    sc_n = jnp.sum(w * rel_n, axis=0, keepdims=True)
    past = npages * PAGE_SIZE
    o_ref[0, :, past:past + PAGE_SIZE] = jnp.where(_iota((1, PAGE_SIZE), 1) == 0, sc_n, NEG_INF)


def dsa_sample_scores(page_table, iq, w_col, ikn, cik, layer):
    ns, npages = page_table.shape
    ncol = (npages + 1) * PAGE_SIZE
    gs = pltpu.PrefetchScalarGridSpec(
        num_scalar_prefetch=1,
        grid=(ns,),
        in_specs=[pl.BlockSpec((1, N_IDX_HEADS, IDX_DIM), lambda b, pt: (b, 0, 0)),
                  pl.BlockSpec((1, N_IDX_HEADS, 1), lambda b, pt: (b, 0, 0)),
                  pl.BlockSpec((1, 1, IDX_DIM), lambda b, pt: (b, 0, 0))]
        + _page_specs(npages, layer, (IDX_DIM, PAGE_SIZE)),
        out_specs=pl.BlockSpec((1, 1, ncol), lambda b, pt: (b, 0, 0)),
    )
    return pl.pallas_call(
        functools.partial(_dsa_scores_kernel, npages=npages),
        grid_spec=gs,
        out_shape=jax.ShapeDtypeStruct((ns, 1, ncol), F32),
        compiler_params=_cparams(("parallel",)),
    )(page_table, iq, w_col, ikn, *([cik] * npages))


def _select_kernel(s_ref, o_ref, key_ref, eq_ref, *, n_vis, k_sel):
    vis = _iota(s_ref.shape, 1) < n_vis
    key_ref[...] = _sortable_key(jnp.where(vis, s_ref[...], NEG_INF))
    sel = _select_topk(key_ref, eq_ref, vis, k_sel)
    o_ref[...] = jnp.where(sel, 1.0, 0.0)


def select_rows(scores, n_vis, k_sel):
    return pl.pallas_call(
        functools.partial(_select_kernel, n_vis=n_vis, k_sel=k_sel),
        out_shape=jax.ShapeDtypeStruct(scores.shape, F32),
        scratch_shapes=[pltpu.VMEM(scores.shape, I32), pltpu.VMEM(scores.shape, F32)],
        compiler_params=pltpu.CompilerParams(vmem_limit_bytes=VMEM_LIMIT),
    )(scores)


def _dsa_sample_kernel(pt_ref, q_ref, kn_ref, vn_ref, mk_ref, *rest, npages):
    k_refs = rest[0:npages]
    v_refs = rest[npages:2 * npages]
    o_ref = rest[2 * npages]
    q8 = _head_rows(q_ref[0], HEAD_DIM ** -0.5)
    q8b = q8.astype(BF16)
    past = npages * PAGE_SIZE
    s_new = jnp.sum(q8 * kn_ref[0], axis=-1, keepdims=True)
    s_new = jnp.where(mk_ref[0, :, past:past + 1] > 0.0, s_new, NEG_INF)
    ss = []
    m = s_new
    for p in range(npages):
        s = _dot(q8b, k_refs[p][0, 0].astype(BF16))
        s = jnp.where(mk_ref[0, :, p * PAGE_SIZE:(p + 1) * PAGE_SIZE] > 0.0, s, NEG_INF)
        ss.append(s)
        m = jnp.maximum(m, jnp.max(s, axis=-1, keepdims=True))
    pn = jnp.exp(s_new - m)
    l = pn
    acc = pn * vn_ref[0]
    for p in range(npages):
        e = jnp.exp(ss[p] - m)
        l = l + jnp.sum(e, axis=-1, keepdims=True)
        acc = acc + _dot_nt(e.astype(BF16), v_refs[p][0, 0].astype(BF16))
    o_ref[0] = _merge_heads(acc / l)


def dsa_sample(page_table, q, kn, vn, mask, ck, cv, layer):
    ns, npages = page_table.shape
    ncol = mask.shape[-1]
    row = pl.BlockSpec((1, 1, HW), lambda b, pt: (b, 0, 0))
    gs = pltpu.PrefetchScalarGridSpec(
        num_scalar_prefetch=1,
        grid=(ns,),
        in_specs=[row, row, row, pl.BlockSpec((1, 1, ncol), lambda b, pt: (b, 0, 0))]
        + _page_specs(npages, layer, (HW, PAGE_SIZE))
        + _page_specs(npages, layer, (HW, PAGE_SIZE)),
        out_specs=row,
    )
    return pl.pallas_call(
        functools.partial(_dsa_sample_kernel, npages=npages),
        grid_spec=gs,
        out_shape=jax.ShapeDtypeStruct((ns, 1, HW), F32),
        compiler_params=_cparams(("parallel",)),
    )(page_table, q, kn, vn, mask, *([ck] * npages), *([cv] * npages))


def _outproj_kernel(x_ref, fo_ref, do_ref, ho_ref, w_ref, g_ref, b_ref, o_ref):
    sub = (_dot(fo_ref[...].astype(BF16), w_ref[0:HW, :])
           + _dot(do_ref[...].astype(BF16), w_ref[HW:2 * HW, :])
           + _dot(ho_ref[...].astype(BF16), w_ref[2 * HW:, :]))
    o_ref[...] = _layer_norm(ALPHA * x_ref[...] + sub, g_ref[...], b_ref[...])


def outproj_norm(x_all, fo, do, ho, w_out, g, b, tm):
    n = x_all.shape[0]
    rowspec = lambda w: pl.BlockSpec((tm, w), lambda i: (i, 0))
    const = lambda shp: pl.BlockSpec(shp, lambda i: (0, 0))
    return pl.pallas_call(
        _outproj_kernel,
        grid=(n // tm,),
        in_specs=[rowspec(D_MODEL), rowspec(HW), rowspec(HW), rowspec(HGW),
                  const((D_MODEL, D_MODEL)), const((1, D_MODEL)), const((1, D_MODEL))],
        out_specs=rowspec(D_MODEL),
        out_shape=jax.ShapeDtypeStruct((n, D_MODEL), F32),
        compiler_params=_cparams(("parallel",)),
    )(x_all, fo, do, ho, w_out, g, b)


def _moe_kernel(x_ref, wrh_ref, wrl_ref, br_ref, w1_ref, w3_ref, w2_ref, g_ref, b_ref, o_ref,
                xb_scr, gate_scr, acc_scr):
    e = pl.program_id(1)
    tm = x_ref.shape[0]
    lane = _iota((tm, 128), 1)

    @pl.when(e == 0)
    def _():
        x = x_ref[...]
        xh = x.astype(BF16)
        xl = (x - xh.astype(F32)).astype(BF16)
        logits = (_dot(xh, wrh_ref[...]) + _dot(xl, wrh_ref[...]) + _dot(xh, wrl_ref[...])
                  + br_ref[...])
        gl = jnp.where(lane < N_GROUPS, logits, NEG_INF)
        gmax = jnp.max(gl, axis=-1, keepdims=True)
        p_top = 1.0 / jnp.sum(jnp.exp(gl - gmax), axis=-1, keepdims=True)
        lanef = lane.astype(F32)
        grp = jnp.min(jnp.where(gl == gmax, lanef, 1024.0), axis=-1, keepdims=True)
        lo = N_GROUPS + EPG * grp
        ev = jnp.where((lanef >= lo) & (lanef < lo + EPG), logits, NEG_INF)
        v1 = jnp.max(ev, axis=-1, keepdims=True)
        i1 = jnp.min(jnp.where(ev == v1, lanef, 1024.0), axis=-1, keepdims=True)
        ev2 = jnp.where(lanef == i1, NEG_INF, ev)
        v2 = jnp.max(ev2, axis=-1, keepdims=True)
        i2 = jnp.min(jnp.where(ev2 == v2, lanef, 1024.0), axis=-1, keepdims=True)
        t = jnp.exp(v2 - v1)
        g1 = p_top / (1.0 + t)
        g2 = p_top * t / (1.0 + t)
        gate_scr[...] = jnp.where(lanef == i1, g1, 0.0) + jnp.where(lanef == i2, g2, 0.0)
        xb_scr[...] = xh
        acc_scr[...] = jnp.zeros(acc_scr.shape, F32)

    xb = xb_scr[...]
    a = _dot(xb, w1_ref[0])
    h = (a / (1.0 + jnp.exp(-a))) * _dot(xb, w3_ref[0])
    y = _dot(h.astype(BF16), w2_ref[0])
    ge = jnp.sum(jnp.where(lane == e + N_GROUPS, gate_scr[...], 0.0), axis=-1, keepdims=True)
    acc_scr[...] += y * ge

    @pl.when(e == N_EXPERTS - 1)
    def _():
        o_ref[...] = _layer_norm(ALPHA * x_ref[...] + acc_scr[...], g_ref[...], b_ref[...])


def moe_norm(x_all, wr_hi, wr_lo, br, w1, w3, w2, g, b, tm):
    n = x_all.shape[0]
    const = lambda shp: pl.BlockSpec(shp, lambda i, e: (0, 0))
    return pl.pallas_call(
        _moe_kernel,
        grid=(n // tm, N_EXPERTS),
        in_specs=[pl.BlockSpec((tm, D_MODEL), lambda i, e: (i, 0)),
                  const((D_MODEL, 128)), const((D_MODEL, 128)), const((1, 128)),
                  pl.BlockSpec((1, D_MODEL, D_FF), lambda i, e: (e, 0, 0)),
                  pl.BlockSpec((1, D_MODEL, D_FF), lambda i, e: (e, 0, 0)),
                  pl.BlockSpec((1, D_FF, D_MODEL), lambda i, e: (e, 0, 0)),
                  const((1, D_MODEL)), const((1, D_MODEL))],
        out_specs=pl.BlockSpec((tm, D_MODEL), lambda i, e: (i, 0)),
        out_shape=jax.ShapeDtypeStruct((n, D_MODEL), F32),
        scratch_shapes=[pltpu.VMEM((tm, D_MODEL), BF16), pltpu.VMEM((tm, 128), F32),
                        pltpu.VMEM((tm, D_MODEL), F32)],
        compiler_params=_cparams(("parallel", "arbitrary")),
    )(x_all, wr_hi, wr_lo, br, w1, w3, w2, g, b)


def router_weights(wg, bg, we, be):
    w = jnp.concatenate([wg, we, jnp.zeros((D_MODEL, 128 - N_GROUPS - N_EXPERTS), F32)], -1)
    hi = w.astype(BF16)
    lo = (w - hi.astype(F32)).astype(BF16)
    br = jnp.concatenate([bg, be, jnp.zeros((128 - N_GROUPS - N_EXPERTS,), F32)])[None, :]
    return hi, lo, br


def forward(x_prompt, x_sample, cache_fox_k, cache_fox_v, cache_fox_logf, cache_dsa_k, cache_dsa_v,
            cache_idx_k, state_hgrn, page_table, w_in, b_fox, hg_lb, hg_norm, w_out, ln1_g, ln1_b,
            moe_wg, moe_bg, moe_we, moe_be, moe_w1, moe_w3, moe_w2, ln2_g, ln2_b,
            *, tm, tq_fox, tq_dsa, chunk, tm_moe):
    nb, seq, _ = x_prompt.shape
    ns = x_sample.shape[0]
    depth = w_in.shape[0]
    npr = nb * seq
    n_rows = npr + tm
    npages = page_table.shape[1]
    pool = cache_fox_k.shape[1]

    x = jnp.concatenate([x_prompt.reshape(npr, D_MODEL), x_sample.reshape(ns, D_MODEL),
                         jnp.zeros((tm - ns, D_MODEL), F32)], 0)
    w_p = permute_w_in(w_in)
    t64, t32, ts = rope_tables(seq, tm)
    lb, llb, l1m, om = lb_params(hg_lb)
    pages_t = lambda c: jnp.transpose(c, (0, 1, 3, 4, 2)).reshape(depth, pool, HW, PAGE_SIZE)
    ck_f, cv_f, ck_d, cv_d = (pages_t(c) for c in (cache_fox_k, cache_fox_v, cache_dsa_k,
                                                   cache_dsa_v))
    cik_t = jnp.swapaxes(cache_idx_k, 2, 3)
    clf_t = jnp.swapaxes(cache_fox_logf, 2, 3)
    state_t = jnp.transpose(state_hgrn, (0, 2, 3, 4, 1))
    w_out_b = w_out.astype(BF16)
    w1_b, w3_b, w2_b = moe_w1.astype(BF16), moe_w3.astype(BF16), moe_w2.astype(BF16)
    k_sel_s = min(TOPK_MAX, (npages * PAGE_SIZE + 1) // 4)

    outs_p = [[] for _ in range(7)]
    outs_s = [[] for _ in range(7)]
    for l in range(depth):
        bf_row = jnp.concatenate([b_fox[l], jnp.zeros((128 - H_FOX,), F32)])[None, :]
        p_all = project(x, w_p[l], t64, t32, ts, bf_row, tm, seq // tm, npr // tm)
        sm_p = p_all[:npr, C_SM:C_SM + 128]
        ps = p_all[npr:npr + ns]
        sm_s = ps[:, C_SM:C_SM + 128]

        lf_p = sm_p[:, 0:H_FOX].reshape(nb, seq, H_FOX)
        lf_t = jnp.concatenate([jnp.swapaxes(lf_p, 1, 2), jnp.zeros((nb, 8 - H_FOX, seq), F32)], 1)
        fcum = cumsum_rows(lf_t.reshape(nb * 8, seq)).reshape(nb, 8, seq // tq_fox, tq_fox)
        fcum = jnp.transpose(fcum, (0, 2, 1, 3))
        fo_p = fox_prompt(p_all, fcum, nb, seq, tq_fox)
        do_p = dsa_prompt(p_all, nb, seq, tq_dsa)
        par = jnp.concatenate([llb[l:l + 1], l1m[l:l + 1], om[l:l + 1], hg_norm[l][None, :],
                               jnp.zeros((4, HGW), F32)], 0)
        ho_p, st_p = hgrn_prompt(p_all, par, nb, seq, chunk)

        r3 = lambda a: a.reshape(ns, 1, a.shape[-1])
        lfn = jnp.concatenate([sm_s[:, 0:H_FOX], jnp.zeros((ns, 8 - H_FOX), F32)], -1)
        lfn_col = jnp.tile(lfn, (1, npages))[:, :, None]
        fo_s = fox_sample(page_table, r3(ps[:, C_FQ:C_FQ + HW]), r3(ps[:, C_FK:C_FK + HW]),
                          r3(ps[:, C_FV:C_FV + HW]), lfn_col, ck_f, cv_f, clf_t, l)
        iq_s = ps[:, C_IQ:C_IQ + HW].reshape(ns, N_IDX_HEADS, IDX_DIM)
        w_col = sm_s[:, SM_IW:SM_IW + N_IDX_HEADS][:, :, None]
        ik_s = sm_s[:, SM_IK:SM_IK + IDX_DIM]
        sc = dsa_sample_scores(page_table, iq_s, w_col, r3(ik_s), cik_t, l)
        mask = select_rows(sc.reshape(ns, -1), npages * PAGE_SIZE + 1, k_sel_s)
        do_s = dsa_sample(page_table, r3(ps[:, C_DQ:C_DQ + HW]), r3(ps[:, C_DK:C_DK + HW]),
                          r3(ps[:, C_DV:C_DV + HW]), mask[:, None, :], ck_d, cv_d, l)
        tr = lambda c: ps[:, c:c + HGW].T.reshape(H_HG, HG_D, ns)
        par_col = jnp.stack([llb[l], l1m[l], om[l]]).reshape(3, H_HG, HG_D, 1)
        ho_t, st_t = hgrn_sample(tr(C_HQ), tr(C_HF), tr(C_HI), tr(C_HG), state_t, l, par_col,
                                 hg_norm[l].reshape(H_HG, HG_D, 1))
        ho_s = ho_t.reshape(HGW, ns).T
        st_s = jnp.transpose(st_t, (3, 0, 1, 2))

        padr = lambda a: jnp.zeros((tm - ns, a.shape[-1]), F32)
        cat = lambda a, s: jnp.concatenate([a, s, padr(a)], 0)
        fo = cat(fo_p, fo_s.reshape(ns, HW))
        do = cat(do_p, do_s.reshape(ns, HW))
        ho = cat(ho_p, ho_s.reshape(ns, HGW))
        x = outproj_norm(x, fo, do, ho, w_out_b[l], ln1_g[l][None, :], ln1_b[l][None, :], tm)
        wr_hi, wr_lo, br = router_weights(moe_wg[l], moe_bg[l], moe_we[l], moe_be[l])
        x = moe_norm(x, wr_hi, wr_lo, br, w1_b[l], w3_b[l], w2_b[l],
                     ln2_g[l][None, :], ln2_b[l][None, :], tm_moe)

        pp = p_all[:npr]
        hp = lambda a: a.reshape(nb, seq, 4, HEAD_DIM)
        for lst, a in zip(outs_p, (hp(pp[:, C_FK:C_FK + HW]), hp(pp[:, C_FV:C_FV + HW]), lf_p,
                                   hp(pp[:, C_DK:C_DK + HW]), hp(pp[:, C_DV:C_DV + HW]),
                                   sm_p[:, SM_IK:SM_IK + IDX_DIM].reshape(nb, seq, IDX_DIM), st_p)):
            lst.append(a)
        hs = lambda a: a.reshape(ns, 1, 4, HEAD_DIM)
        for lst, a in zip(outs_s, (hs(ps[:, C_FK:C_FK + HW]), hs(ps[:, C_FV:C_FV + HW]),
                                   sm_s[:, 0:H_FOX].reshape(ns, 1, H_FOX),
                                   hs(ps[:, C_DK:C_DK + HW]), hs(ps[:, C_DV:C_DV + HW]),
                                   ik_s.reshape(ns, 1, IDX_DIM), st_s)):
            lst.append(a)

    y_p = x[:npr].reshape(nb, seq, D_MODEL)
    y_s = x[npr:npr + ns].reshape(ns, 1, D_MODEL)
    return (y_p, y_s) + tuple(jnp.stack(a) for a in outs_p) + tuple(jnp.stack(a) for a in outs_s)


def kernel(x_prompt, x_sample, cache_fox_k, cache_fox_v, cache_fox_logf, cache_dsa_k, cache_dsa_v, cache_idx_k, state_hgrn, page_table, w_in, b_fox, hg_lb, hg_norm, w_out, ln1_g, ln1_b, moe_wg, moe_bg, moe_we, moe_be, moe_w1, moe_w3, moe_w2, ln2_g, ln2_b):
    return forward(x_prompt, x_sample, cache_fox_k, cache_fox_v, cache_fox_logf, cache_dsa_k,
                   cache_dsa_v, cache_idx_k, state_hgrn, page_table, w_in, b_fox, hg_lb, hg_norm,
                   w_out, ln1_g, ln1_b, moe_wg, moe_bg, moe_we, moe_be, moe_w1, moe_w3, moe_w2,
                   ln2_g, ln2_b, tm=256, tq_fox=256, tq_dsa=256, chunk=128, tm_moe=1280)
```

```python
import functools

import numpy as np
import jax
import jax.numpy as jnp
from jax import lax
from jax.experimental import pallas as pl
from jax.experimental.pallas import tpu as pltpu

F32 = jnp.float32
BF16 = jnp.bfloat16
I32 = jnp.int32

D_MODEL = 1024
DEPTH = 4
HEAD_DIM = 64
H_FOX = 4
H_DSA = 4
H_HG = 8
HG_D = 64
N_IDX_HEADS = 8
IDX_DIM = 32
PAGE_SIZE = 128
TOPK_MAX = 256
ROPE_THETA = 500000.0
N_GROUPS = 4
EPG = 4
N_EXPERTS = 16
D_FF = 512
LN_EPS = 1e-5
ALPHA = (2 * DEPTH) ** 0.25
HW = 256
HGW = 512

C_HQ, C_HF, C_HI, C_HG = 0, 512, 1024, 1536
C_FQ, C_FK, C_FV, C_DQ, C_DK, C_DV, C_IQ = (2048 + i * 256 for i in range(7))
C_SM = 3840
N_P = 3968
SM_IK = 32
SM_IW = 64

VMEM_LIMIT = 56 * 1024 * 1024
NEG_INF = float("-inf")


def _cparams(sem):
    return pltpu.CompilerParams(dimension_semantics=sem, vmem_limit_bytes=VMEM_LIMIT)


def _iota(shape, dim):
    return lax.broadcasted_iota(I32, shape, dim)


def _dot(a, b):
    return jnp.dot(a, b, preferred_element_type=F32)


def _dot_nt(a, b):
    return lax.dot_general(a, b, (((1,), (1,)), ((), ())), preferred_element_type=F32)


def _split3(x):
    h = x.astype(BF16)
    r = x - h.astype(F32)
    m = r.astype(BF16)
    l = (r - m.astype(F32)).astype(BF16)
    return h, m, l


def _dot3_l(x, w):
    h, m, l = _split3(x)
    return _dot(h, w) + _dot(m, w) + _dot(l, w)


def _dot3_r(w, x):
    h, m, l = _split3(x)
    return _dot(w, h) + _dot(w, m) + _dot(w, l)


def _layer_norm(y, g, b):
    mu = jnp.mean(y, axis=-1, keepdims=True)
    yc = y - mu
    var = jnp.mean(yc * yc, axis=-1, keepdims=True)
    return yc * lax.rsqrt(var + LN_EPS) * g + b


def _lb_kernel(x_ref, lb_ref, llb_ref, l1m_ref, om_ref):
    x = x_ref[...]
    m = jnp.max(x, axis=0, keepdims=True)
    e = jnp.exp(x - m)
    p = e / jnp.sum(e, axis=0, keepdims=True)
    n = x.shape[0]
    c = p[0:1]
    c0 = c
    for l in range(n):
        if l > 0:
            c = c + p[l:l + 1]
        lb = c - c0
        lb_ref[l:l + 1, :] = lb
        llb_ref[l:l + 1, :] = jnp.log(lb)
        l1m_ref[l:l + 1, :] = jnp.log1p(-lb)
        om_ref[l:l + 1, :] = 1.0 - lb


def lb_params(hg_lb):
    shp = jax.ShapeDtypeStruct(hg_lb.shape, F32)
    return pl.pallas_call(_lb_kernel, out_shape=(shp, shp, shp, shp))(hg_lb.astype(F32))


def _rope(v, t_ref, half):
    n = v.shape[-1]
    return (v * t_ref[0] + pltpu.roll(v, n - half, 1) * t_ref[1]
            + pltpu.roll(v, half, 1) * t_ref[2])


def _proj_kernel(x_ref, w_ref, t64_ref, t32_ref, ts_ref, bf_ref,
                 o_ref, fkt_ref, fvt_ref, dkt_ref, dvt_ref, ikt_ref, lft_ref):
    x = x_ref[...].astype(BF16)

    def seg(a, wd):
        return _dot(x, w_ref[:, a:a + wd])

    o_ref[:, C_HQ:C_HQ + 1024] = seg(C_HQ, 1024)
    o_ref[:, C_HI:C_HI + 1024] = seg(C_HI, 1024)
    o_ref[:, C_FQ:C_FQ + HW] = seg(C_FQ, HW)
    o_ref[:, C_DQ:C_DQ + HW] = _rope(seg(C_DQ, HW), t64_ref, 8)
    o_ref[:, C_IQ:C_IQ + HW] = _rope(seg(C_IQ, HW), t32_ref, 4)
    for c0, t_ref, rot in ((C_FK, fkt_ref, False), (C_FV, fvt_ref, False),
                           (C_DK, dkt_ref, True), (C_DV, dvt_ref, False)):
        v = seg(c0, HW)
        if rot:
            v = _rope(v, t64_ref, 8)
        o_ref[:, c0:c0 + HW] = v
        t_ref[0] = v.T
    sm = _rope(seg(C_SM, 128), ts_ref, 4)
    z = sm + bf_ref[...]
    logsig = jnp.minimum(z, 0.0) - jnp.log1p(jnp.exp(-jnp.abs(z)))
    sm = jnp.where(_iota(sm.shape, 1) < H_FOX, logsig, sm)
    o_ref[:, C_SM:C_SM + 128] = sm
    smt = sm.T
    lft_ref[0] = jnp.where(_iota((8, smt.shape[1]), 0) < H_FOX, smt[0:8], 0.0)
    ikt_ref[0] = smt[SM_IK:SM_IK + IDX_DIM]


def project(x, w_p, t64, t32, ts, bfox_row, tm, nb, seq, tab_tile):
    tps = seq // tm
    tab_map = lambda i: (0, tab_tile(i), 0)
    tspec = lambda w: pl.BlockSpec((1, w, tm), lambda i: (i // tps, 0, i % tps))
    tshape = lambda w: jax.ShapeDtypeStruct((nb, w, seq), F32)
    return pl.pallas_call(
        _proj_kernel,
        grid=(nb * tps,),
        in_specs=[
            pl.BlockSpec((tm, D_MODEL), lambda i: (i, 0)),
            pl.BlockSpec((D_MODEL, N_P), lambda i: (0, 0)),
            pl.BlockSpec((3, tm, HW), tab_map),
            pl.BlockSpec((3, tm, HW), tab_map),
            pl.BlockSpec((3, tm, 128), tab_map),
            pl.BlockSpec((1, 128), lambda i: (0, 0)),
        ],
        out_specs=[pl.BlockSpec((tm, N_P), lambda i: (i, 0)), tspec(HW), tspec(HW), tspec(HW),
                   tspec(HW), tspec(IDX_DIM), tspec(8)],
        out_shape=[jax.ShapeDtypeStruct((nb * seq, N_P), F32), tshape(HW), tshape(HW),
                   tshape(HW), tshape(HW), tshape(IDX_DIM), tshape(8)],
        compiler_params=_cparams(("parallel",)),
    )(x, w_p, t64, t32, ts, bfox_row)


def rope_tables(seq, tm):
    pos = jnp.concatenate([jnp.arange(seq), jnp.full((tm,), seq)]).astype(F32)

    def head_tabs(hd):
        rot = hd // 4
        half = rot // 2
        inv = ROPE_THETA ** (-jnp.arange(half, dtype=F32) * 2.0 / rot)
        ang = pos[:, None] * inv[None, :]
        cos, sin = jnp.cos(ang), jnp.sin(ang)
        n = pos.shape[0]
        one = jnp.ones((n, hd - rot), F32)
        zero = jnp.zeros((n, hd - rot), F32)
        zh = jnp.zeros((n, half), F32)
        c = jnp.concatenate([cos, cos, one], -1)
        sa = jnp.concatenate([-sin, zh, zero], -1)
        sb = jnp.concatenate([zh, sin, zero], -1)
        return c, sa, sb

    t64 = jnp.stack([jnp.tile(a, (1, HW // 64)) for a in head_tabs(64)])
    h32 = head_tabs(32)
    t32 = jnp.stack([jnp.tile(a, (1, HW // 32)) for a in h32])
    n = pos.shape[0]
    fill = [jnp.ones, jnp.zeros, jnp.zeros]
    ts = jnp.stack([jnp.concatenate([f((n, SM_IK), F32), a, f((n, 128 - SM_IK - 32), F32)], -1)
                    for f, a in zip(fill, h32)])
    return t64, t32, ts


def permute_w_in(w_in):
    L = w_in.shape[0]
    z = lambda n: jnp.zeros((L, D_MODEL, n), w_in.dtype)
    ik = w_in[:, :, 1796:1828]
    small = jnp.concatenate([w_in[:, :, 768:772], z(SM_IK - 4), ik, w_in[:, :, 1828:1836],
                             z(128 - SM_IW - 8)], -1)
    w = jnp.concatenate([w_in[:, :, 1836:3884], w_in[:, :, 0:768], w_in[:, :, 772:1540],
                         w_in[:, :, 1540:1796], small], -1)
    return w.astype(BF16)


def _cumsum_kernel(x_ref, tri_ref, o_ref):
    r, n = x_ref.shape
    carry = jnp.zeros((r, 1), F32)
    tri = tri_ref[...]
    for c in range(n // 128):
        blk = x_ref[:, c * 128:(c + 1) * 128]
        cs = _dot3_l(blk, tri) + carry
        o_ref[:, c * 128:(c + 1) * 128] = cs
        carry = carry + jnp.sum(blk, axis=-1, keepdims=True)


def cumsum_rows(x):
    tri = jnp.asarray(np.triu(np.ones((128, 128), np.float32)), BF16)
    return pl.pallas_call(_cumsum_kernel, out_shape=jax.ShapeDtypeStruct(x.shape, F32))(x, tri)


def _fox_prompt_kernel(q_ref, kt_ref, vt_ref, f_ref, o_ref, *, tq, qi):
    nvis = (qi + 1) * tq
    q = q_ref[...] * (HEAD_DIM ** -0.5)
    causal = _iota((tq, nvis), 1) <= _iota((tq, nvis), 0) + qi * tq
    outs = []
    for h in range(H_FOX):
        hs = slice(h * HEAD_DIM, (h + 1) * HEAD_DIM)
        s = _dot(q[:, hs].astype(BF16), kt_ref[0, hs, :].astype(BF16)) - f_ref[0, h:h + 1, :]
        s = jnp.where(causal, s, NEG_INF)
        p = jnp.exp(s - jnp.max(s, axis=-1, keepdims=True))
        l = jnp.sum(p, axis=-1, keepdims=True)
        outs.append(_dot_nt(p.astype(BF16), vt_ref[0, hs, :].astype(BF16)) / l)
    o_ref[0] = jnp.concatenate(outs, axis=-1)


def _stack_blocks(outs, nb, seq):
    return jnp.stack(outs, axis=1).reshape(nb * seq, outs[0].shape[-1])


def fox_prompt(p_all, fkt, fvt, fcum, nb, seq, tq):
    nq = seq // tq
    outs = []
    for qi in range(nq):
        nvis = (qi + 1) * tq
        kspec = lambda w: pl.BlockSpec((1, w, nvis), lambda b: (b, 0, 0))
        outs.append(pl.pallas_call(
            functools.partial(_fox_prompt_kernel, tq=tq, qi=qi),
            grid=(nb,),
            in_specs=[pl.BlockSpec((tq, HW), lambda b, qi=qi: (b * nq + qi, C_FQ // HW)),
                      kspec(HW), kspec(HW), kspec(8)],
            out_specs=pl.BlockSpec((1, tq, HW), lambda b: (b, 0, 0)),
            out_shape=jax.ShapeDtypeStruct((nb, tq, HW), F32),
            compiler_params=_cparams(("parallel",)),
        )(p_all, fkt, fvt, fcum))
    return _stack_blocks(outs, nb, seq)


def _sortable_key(score):
    bits = lax.bitcast_convert_type(score + 0.0, I32)
    return jnp.where(bits < 0, bits ^ jnp.int32(0x7FFFFFFF), bits)


def _kth_largest_key(key_ref, half_ref, k):
    rows, cols = key_ref.shape
    kf = float(k)
    i16 = jnp.int16
    one, zero = jnp.ones((), BF16), jnp.zeros((), BF16)

    def count(cmp, cand):
        cand16 = jnp.broadcast_to(cand, (rows, 128)).astype(i16)
        acc = jnp.zeros((rows, 128), BF16)
        for c in range(cols // 128):
            acc = acc + jnp.where(cmp(half_ref[:, c * 128:(c + 1) * 128], cand16), one, zero)
        return jnp.sum(acc.astype(F32), axis=-1, keepdims=True)

    def search(base):
        def body(i, t):
            cand = t + jnp.left_shift(jnp.int32(1), 15 - i)
            return jnp.where(base + count(lambda a, b: a >= b, cand) >= kf, cand, t)

        return lax.fori_loop(0, 16, body, jnp.full((rows, 1), -2 ** 15, I32))

    key = key_ref[...]
    hi = key >> 16
    half_ref[...] = hi.astype(i16)
    t_hi = search(0.0)
    above = count(lambda a, b: a > b, t_hi)
    half_ref[...] = jnp.where(hi == t_hi, (key & 0xFFFF) - 2 ** 15, -2 ** 15).astype(i16)
    t_lo = search(above)
    return t_hi * 65536 + (t_lo + 2 ** 15)


def _select_topk(key_ref, eq_ref, half_ref, vis, k):
    rows, cols = key_ref.shape
    kf = float(k)
    thr = _kth_largest_key(key_ref, half_ref, k)
    key = key_ref[...]
    gt = key > thr
    eqf = jnp.where(key == thr, 1.0, 0.0)
    eq_ref[...] = eqf
    need = kf - jnp.sum(jnp.where(gt, 1.0, 0.0), axis=-1, keepdims=True)
    n_eq = jnp.sum(eqf, axis=-1, keepdims=True)
    nbits = int(cols).bit_length()
    tied = (n_eq > need) & (thr > _sortable_key(jnp.full((1, 1), NEG_INF, F32)))
    any_tied = jnp.max(jnp.where(tied, 1.0, 0.0)) > 0.0

    def search():
        def body2(i, x):
            cand = x + jnp.left_shift(jnp.int32(1), nbits - 1 - i)
            col = _iota((rows, cols), 1)
            c = jnp.sum(jnp.where(col < cand, eq_ref[...], 0.0), axis=-1, keepdims=True)
            return jnp.where(c < need, cand, x)

        return lax.fori_loop(0, nbits, body2, jnp.zeros((rows, 1), I32))

    x = lax.cond(any_tied, search, lambda: jnp.full((rows, 1), cols, I32))
    x = jnp.where(tied, x, cols)
    col = _iota((rows, cols), 1)
    return vis & (gt | ((eq_ref[...] > 0.0) & (col <= x)))


def _dsa_prompt_kernel(dq_ref, iq_ref, sm_ref, dkt_ref, dvt_ref, ikt_ref, o_ref, key_ref, eq_ref,
                       half_ref, *, tq, k_sel, qi):
    nvis = (qi + 1) * tq
    iq = iq_ref[...] * (IDX_DIM ** -0.5)
    ikt = ikt_ref[0].astype(BF16)
    sm = sm_ref[...]
    score = jnp.zeros((tq, nvis), F32)
    for h in range(N_IDX_HEADS):
        rel = jnp.maximum(_dot(iq[:, h * IDX_DIM:(h + 1) * IDX_DIM].astype(BF16), ikt), 0.0)
        w = sm[:, SM_IW + h:SM_IW + h + 1] * (N_IDX_HEADS ** -0.5)
        score = score + w * rel
    vis = _iota((tq, nvis), 1) <= _iota((tq, nvis), 0) + qi * tq
    key_ref[...] = _sortable_key(jnp.where(vis, score, NEG_INF))
    sel = _select_topk(key_ref, eq_ref, half_ref, vis, k_sel)

    dq = dq_ref[...] * (HEAD_DIM ** -0.5)
    outs = []
    for h in range(H_DSA):
        hs = slice(h * HEAD_DIM, (h + 1) * HEAD_DIM)
        s = _dot(dq[:, hs].astype(BF16), dkt_ref[0, hs, :].astype(BF16))
        s = jnp.where(sel, s, NEG_INF)
        p = jnp.exp(s - jnp.max(s, axis=-1, keepdims=True))
        l = jnp.sum(p, axis=-1, keepdims=True)
        outs.append(_dot_nt(p.astype(BF16), dvt_ref[0, hs, :].astype(BF16)) / l)
    o_ref[0] = jnp.concatenate(outs, axis=-1)


def dsa_prompt(p_all, dkt, dvt, ikt, nb, seq, tq):
    nq = seq // tq
    k_sel = min(TOPK_MAX, seq // 4)
    outs = []
    for qi in range(nq):
        nvis = (qi + 1) * tq
        kspec = lambda w: pl.BlockSpec((1, w, nvis), lambda b: (b, 0, 0))
        qspec = lambda c, w, qi=qi: pl.BlockSpec((tq, w), lambda b: (b * nq + qi, c))
        outs.append(pl.pallas_call(
            functools.partial(_dsa_prompt_kernel, tq=tq, k_sel=k_sel, qi=qi),
            grid=(nb,),
            in_specs=[qspec(C_DQ // HW, HW), qspec(C_IQ // HW, HW), qspec(C_SM // 128, 128),
                      kspec(HW), kspec(HW), kspec(IDX_DIM)],
            out_specs=pl.BlockSpec((1, tq, HW), lambda b: (b, 0, 0)),
            out_shape=jax.ShapeDtypeStruct((nb, tq, HW), F32),
            scratch_shapes=[pltpu.VMEM((tq, nvis), I32), pltpu.VMEM((tq, nvis), F32),
                            pltpu.VMEM((tq, nvis), jnp.int16)],
            compiler_params=_cparams(("parallel",)),
        )(p_all, p_all, p_all, dkt, dvt, ikt))
    return _stack_blocks(outs, nb, seq)


def hgrn_level_mats(c):
    nl = int(np.log2(c))
    t = np.arange(c)
    mats = [(t[None, :] <= t[:, None]).astype(np.float32)]
    masks = []
    for L in range(nl):
        bit = (t >> L) & 1
        lo = (t >> L) << L
        w = np.zeros((c, c), np.float32)
        for r in range(c):
            if bit[r]:
                w[r, lo[r]:r + 1] = 1.0
            else:
                w[r, r + 1:lo[r] + (1 << L)] = 1.0
        mats.append(w)
        same = (t[:, None] >> (L + 1)) == (t[None, :] >> (L + 1))
        masks.append((bit[:, None] == 1) & (bit[None, :] == 0) & same)
    return np.stack(mats), np.stack(masks).astype(np.float32)


def _hgrn_gates(fl, llb, l1m, om):
    ls = jnp.minimum(fl, 0.0) - jnp.log1p(jnp.exp(-jnp.abs(fl)))
    b = l1m + ls
    logf = jnp.maximum(llb, b) + jnp.log1p(jnp.exp(-jnp.abs(llb - b)))
    kk = om / (1.0 + jnp.exp(fl))
    return logf, kk


def _hgrn_prompt_kernel(q_ref, f_ref, v_ref, g_ref, par_ref, wl_ref, ml_ref, bd_ref, o_ref, st_ref,
                        s_scr, *, c, nl):
    ci = pl.program_id(1)

    @pl.when(ci == 0)
    def _():
        s_scr[...] = jnp.zeros(s_scr.shape, F32)

    q = q_ref[...]
    v = v_ref[...]
    logf, kk = _hgrn_gates(f_ref[...], par_ref[0:1, :], par_ref[1:2, :], par_ref[2:3, :])
    lf3 = jnp.concatenate(_split3(logf), axis=-1)

    e_all = _dot(wl_ref[...], lf3)

    def rowsum(i):
        e = e_all[i * c:(i + 1) * c]
        return e[:, 0:HGW] + e[:, HGW:2 * HGW] + e[:, 2 * HGW:3 * HGW]

    bd = bd_ref[...]
    vb = v.astype(BF16)
    row = _iota((c, HGW), 0)
    a_heads = [jnp.zeros((c, c), F32) for _ in range(H_HG)]
    for L in range(nl):
        e = rowsum(1 + L)
        x = (jnp.where(((row >> L) & 1) == 1, q, kk) * jnp.exp(e)).astype(BF16)
        msk = ml_ref[L] > 0.0
        for h in range(H_HG):
            xh = x[:, h * HG_D:(h + 1) * HG_D]
            a_heads[h] = a_heads[h] + jnp.where(msk, _dot_nt(xh, xh), 0.0)
    o = _dot((q * kk).astype(BF16), bd) * v
    o = o + jnp.concatenate(
        [_dot(a_heads[h].astype(BF16), vb[:, h * HG_D:(h + 1) * HG_D]) for h in range(H_HG)],
        axis=-1)
    g = rowsum(0)
    st = s_scr[...]
    o = o + _dot_nt((q * jnp.exp(g)).astype(BF16), st.astype(BF16))
    gl = g[c - 1:c, :]
    kd = (kk * jnp.exp(gl - g)).astype(BF16)
    upd = _dot(v.T.astype(BF16), kd)
    s_new = st * jnp.exp(gl) + jnp.where(bd > 0, upd, 0.0)
    s_scr[...] = s_new
    st_ref[0] = s_new

    o2 = o * o
    o2h = o2.astype(BF16)
    o2l = (o2 - o2h.astype(F32)).astype(BF16)
    ms = (_dot(o2h, bd) + _dot(o2l, bd)) * (1.0 / HG_D)
    hg = g_ref[...]
    o_ref[...] = o * lax.rsqrt(ms + LN_EPS) * par_ref[3:4, :] * (hg / (1.0 + jnp.exp(-hg)))


def hgrn_prompt(p_all, par, nb, seq, c):
    nc = seq // c
    nl = int(np.log2(c))
    wl, ml = hgrn_level_mats(c)
    bd = np.kron(np.eye(H_HG, dtype=np.float32), np.ones((HG_D, HG_D), np.float32))
    o, st = pl.pallas_call(
        functools.partial(_hgrn_prompt_kernel, c=c, nl=nl),
        grid=(nb, nc),
        in_specs=[
            pl.BlockSpec((c, HGW), lambda b, i: (b * nc + i, C_HQ // HGW)),
            pl.BlockSpec((c, HGW), lambda b, i: (b * nc + i, C_HF // HGW)),
            pl.BlockSpec((c, HGW), lambda b, i: (b * nc + i, C_HI // HGW)),
            pl.BlockSpec((c, HGW), lambda b, i: (b * nc + i, C_HG // HGW)),
            pl.BlockSpec((8, HGW), lambda b, i: (0, 0)),
            pl.BlockSpec(((nl + 1) * c, c), lambda b, i: (0, 0)),
            pl.BlockSpec((nl, c, c), lambda b, i: (0, 0, 0)),
            pl.BlockSpec((HGW, HGW), lambda b, i: (0, 0)),
        ],
        out_specs=[
            pl.BlockSpec((c, HGW), lambda b, i: (b * nc + i, 0)),
            pl.BlockSpec((1, HGW, HGW), lambda b, i: (b, 0, 0)),
        ],
        out_shape=[jax.ShapeDtypeStruct((nb * seq, HGW), F32),
                   jax.ShapeDtypeStruct((nb, HGW, HGW), F32)],
        scratch_shapes=[pltpu.VMEM((HGW, HGW), F32)],
        compiler_params=_cparams(("parallel", "arbitrary")),
    )(p_all, p_all, p_all, p_all, par, jnp.asarray(wl.reshape(-1, c), BF16), jnp.asarray(ml, F32),
      jnp.asarray(bd, BF16))
    ar = jnp.arange(H_HG)
    st = st.reshape(nb, H_HG, HG_D, H_HG, HG_D)[:, ar, :, ar, :]
    return o, jnp.transpose(st, (1, 0, 3, 2))


def _hgrn_sample_kernel(q_ref, f_ref, v_ref, g_ref, s_ref, pc_ref, n_ref, o_ref, so_ref,
                        f_scr, k_scr):
    logf, kk = _hgrn_gates(f_ref[0], pc_ref[0, 0], pc_ref[1, 0], pc_ref[2, 0])
    f_scr[...] = jnp.exp(logf)
    k_scr[...] = kk
    v = v_ref[0]

    def body(d, o):
        sn = f_scr[pl.ds(d, 1), :] * s_ref[0, 0, d] + k_scr[pl.ds(d, 1), :] * v
        so_ref[0, d] = sn
        return o + q_ref[0, pl.ds(d, 1), :] * sn

    o = lax.fori_loop(0, HG_D, body, jnp.zeros(v.shape, F32), unroll=8)
    ms = jnp.mean(o * o, axis=0, keepdims=True)
    hg = g_ref[0]
    o_ref[0] = o * lax.rsqrt(ms + LN_EPS) * n_ref[0] * (hg / (1.0 + jnp.exp(-hg)))


def hgrn_sample(qt, ft, vt, gt, state_t, layer, par_col, norm_col):
    ns = qt.shape[-1]
    vec = pl.BlockSpec((1, HG_D, ns), lambda h: (h, 0, 0))
    return pl.pallas_call(
        _hgrn_sample_kernel,
        grid=(H_HG,),
        in_specs=[vec, vec, vec, vec,
                  pl.BlockSpec((1, 1, HG_D, HG_D, ns), lambda h: (layer, h, 0, 0, 0)),
                  pl.BlockSpec((3, 1, HG_D, 1), lambda h: (0, h, 0, 0)),
                  pl.BlockSpec((1, HG_D, 1), lambda h: (h, 0, 0))],
        out_specs=[vec, pl.BlockSpec((1, HG_D, HG_D, ns), lambda h: (h, 0, 0, 0))],
        out_shape=[jax.ShapeDtypeStruct((H_HG, HG_D, ns), F32),
                   jax.ShapeDtypeStruct((H_HG, HG_D, HG_D, ns), F32)],
        scratch_shapes=[pltpu.VMEM((HG_D, ns), F32), pltpu.VMEM((HG_D, ns), F32)],
        compiler_params=_cparams(("parallel",)),
    )(qt, ft, vt, gt, state_t, par_col, norm_col)


def _head_rows(q, scale):
    qb = jnp.broadcast_to(q * scale, (8, HW))
    return jnp.where((_iota((8, HW), 1) >> 6) == _iota((8, HW), 0), qb, 0.0)


def _merge_heads(acc):
    return jnp.sum(jnp.where((_iota((8, HW), 1) >> 6) == _iota((8, HW), 0), acc, 0.0),
                   axis=0, keepdims=True)


def _fox_sample_kernel(pt_ref, q_ref, kn_ref, vn_ref, lfn_ref, u_ref, mc_ref, *rest, npages):
    k_refs = rest[0:npages]
    v_refs = rest[npages:2 * npages]
    lf_refs = rest[2 * npages:3 * npages]
    o_ref = rest[3 * npages]
    lf_scr = rest[3 * npages + 1]
    q8 = _head_rows(q_ref[0], HEAD_DIM ** -0.5)
    q8b = q8.astype(BF16)
    lf_scr[...] = jnp.zeros(lf_scr.shape, F32)
    for p in range(npages):
        lf_scr[8 * p:8 * p + H_FOX, :] = lf_refs[p][0, 0]
    lf = lf_scr[...]
    tot = jnp.broadcast_to(jnp.sum(lf, axis=-1, keepdims=True), lf.shape)
    bias = _dot3_l(lf, u_ref[...]) + _dot3_r(mc_ref[...], tot) + lfn_ref[0]
    s_new = jnp.sum(q8 * kn_ref[0], axis=-1, keepdims=True)
    ss = []
    m = s_new
    for p in range(npages):
        s = _dot(q8b, k_refs[p][0, 0].astype(BF16)) + bias[8 * p:8 * p + 8, :]
        ss.append(s)
        m = jnp.maximum(m, jnp.max(s, axis=-1, keepdims=True))
    pn = jnp.exp(s_new - m)
    l = pn
    acc = pn * vn_ref[0]
    for p in range(npages):
        e = jnp.exp(ss[p] - m)
        l = l + jnp.sum(e, axis=-1, keepdims=True)
        acc = acc + _dot_nt(e.astype(BF16), v_refs[p][0, 0].astype(BF16))
    o_ref[0] = _merge_heads(acc / l)


def _page_specs(npages, layer, blk):
    def mk(p):
        return pl.BlockSpec((1, 1) + blk, lambda b, pt: (layer, pt[b, p], 0, 0))
    return [mk(p) for p in range(npages)]


def fox_sample(page_table, q, kn, vn, lfn_col, ck, cv, clf_t, layer):
    ns, npages = page_table.shape
    r = 8 * npages
    u = np.tril(np.ones((PAGE_SIZE, PAGE_SIZE), np.float32), -1)
    pg = np.arange(r) // 8
    hd = np.arange(r) % 8
    mc = ((hd[:, None] == hd[None, :]) & (pg[None, :] > pg[:, None])).astype(np.float32)
    row = pl.BlockSpec((1, 1, HW), lambda b, pt: (b, 0, 0))
    gs = pltpu.PrefetchScalarGridSpec(
        num_scalar_prefetch=1,
        grid=(ns,),
        in_specs=[row, row, row,
                  pl.BlockSpec((1, r, 1), lambda b, pt: (b, 0, 0)),
                  pl.BlockSpec((PAGE_SIZE, PAGE_SIZE), lambda b, pt: (0, 0)),
                  pl.BlockSpec((r, r), lambda b, pt: (0, 0))]
        + _page_specs(npages, layer, (HW, PAGE_SIZE))
        + _page_specs(npages, layer, (HW, PAGE_SIZE))
        + _page_specs(npages, layer, (H_FOX, PAGE_SIZE)),
        out_specs=row,
        scratch_shapes=[pltpu.VMEM((r, PAGE_SIZE), F32)],
    )
    return pl.pallas_call(
        functools.partial(_fox_sample_kernel, npages=npages),
        grid_spec=gs,
        out_shape=jax.ShapeDtypeStruct((ns, 1, HW), F32),
        compiler_params=_cparams(("parallel",)),
    )(page_table, q, kn, vn, lfn_col, jnp.asarray(u, BF16), jnp.asarray(mc, BF16),
      *([ck] * npages), *([cv] * npages), *([clf_t] * npages))


def _dsa_scores_kernel(pt_ref, iq_ref, w_ref, ikn_ref, *rest, npages):
    ik_refs = rest[0:npages]
    o_ref = rest[npages]
    iq = iq_ref[0] * (IDX_DIM ** -0.5)
    w = w_ref[0] * (N_IDX_HEADS ** -0.5)
    iqb = iq.astype(BF16)
    for p in range(npages):
        rel = jnp.maximum(_dot(iqb, ik_refs[p][0, 0].astype(BF16)), 0.0)
        o_ref[0, :, p * PAGE_SIZE:(p + 1) * PAGE_SIZE] = jnp.sum(w * rel, axis=0, keepdims=True)
    rel_n = jnp.maximum(jnp.sum(iq * ikn_ref[0], axis=-1, keepdims=True), 0.0)
    sc_n = jnp.sum(w * rel_n, axis=0, keepdims=True)
    past = npages * PAGE_SIZE
    o_ref[0, :, past:past + PAGE_SIZE] = jnp.where(_iota((1, PAGE_SIZE), 1) == 0, sc_n, NEG_INF)


def dsa_sample_scores(page_table, iq, w_col, ikn, cik, layer):
    ns, npages = page_table.shape
    ncol = (npages + 1) * PAGE_SIZE
    gs = pltpu.PrefetchScalarGridSpec(
        num_scalar_prefetch=1,
        grid=(ns,),
        in_specs=[pl.BlockSpec((1, N_IDX_HEADS, IDX_DIM), lambda b, pt: (b, 0, 0)),
                  pl.BlockSpec((1, N_IDX_HEADS, 1), lambda b, pt: (b, 0, 0)),
                  pl.BlockSpec((1, 1, IDX_DIM), lambda b, pt: (b, 0, 0))]
        + _page_specs(npages, layer, (IDX_DIM, PAGE_SIZE)),
        out_specs=pl.BlockSpec((1, 1, ncol), lambda b, pt: (b, 0, 0)),
    )
    return pl.pallas_call(
        functools.partial(_dsa_scores_kernel, npages=npages),
        grid_spec=gs,
        out_shape=jax.ShapeDtypeStruct((ns, 1, ncol), F32),
        compiler_params=_cparams(("parallel",)),
    )(page_table, iq, w_col, ikn, *([cik] * npages))


def _select_kernel(s_ref, o_ref, key_ref, eq_ref, half_ref, *, n_vis, k_sel):
    vis = _iota(s_ref.shape, 1) < n_vis
    key_ref[...] = _sortable_key(jnp.where(vis, s_ref[...], NEG_INF))
    sel = _select_topk(key_ref, eq_ref, half_ref, vis, k_sel)
    o_ref[...] = jnp.where(sel, 1.0, 0.0)


def select_rows(scores, n_vis, k_sel):
    return pl.pallas_call(
        functools.partial(_select_kernel, n_vis=n_vis, k_sel=k_sel),
        out_shape=jax.ShapeDtypeStruct(scores.shape, F32),
        scratch_shapes=[pltpu.VMEM(scores.shape, I32), pltpu.VMEM(scores.shape, F32),
                        pltpu.VMEM(scores.shape, jnp.int16)],
        compiler_params=pltpu.CompilerParams(vmem_limit_bytes=VMEM_LIMIT),
    )(scores)


def _dsa_sample_kernel(pt_ref, q_ref, kn_ref, vn_ref, mk_ref, *rest, npages):
    k_refs = rest[0:npages]
    v_refs = rest[npages:2 * npages]
    o_ref = rest[2 * npages]
    q8 = _head_rows(q_ref[0], HEAD_DIM ** -0.5)
    q8b = q8.astype(BF16)
    past = npages * PAGE_SIZE
    s_new = jnp.sum(q8 * kn_ref[0], axis=-1, keepdims=True)
    s_new = jnp.where(mk_ref[0, :, past:past + 1] > 0.0, s_new, NEG_INF)
    ss = []
    m = s_new
    for p in range(npages):
        s = _dot(q8b, k_refs[p][0, 0].astype(BF16))
        s = jnp.where(mk_ref[0, :, p * PAGE_SIZE:(p + 1) * PAGE_SIZE] > 0.0, s, NEG_INF)
        ss.append(s)
        m = jnp.maximum(m, jnp.max(s, axis=-1, keepdims=True))
    pn = jnp.exp(s_new - m)
    l = pn
    acc = pn * vn_ref[0]
    for p in range(npages):
        e = jnp.exp(ss[p] - m)
        l = l + jnp.sum(e, axis=-1, keepdims=True)
        acc = acc + _dot_nt(e.astype(BF16), v_refs[p][0, 0].astype(BF16))
    o_ref[0] = _merge_heads(acc / l)


def dsa_sample(page_table, q, kn, vn, mask, ck, cv, layer):
    ns, npages = page_table.shape
    ncol = mask.shape[-1]
    row = pl.BlockSpec((1, 1, HW), lambda b, pt: (b, 0, 0))
    gs = pltpu.PrefetchScalarGridSpec(
        num_scalar_prefetch=1,
        grid=(ns,),
        in_specs=[row, row, row, pl.BlockSpec((1, 1, ncol), lambda b, pt: (b, 0, 0))]
        + _page_specs(npages, layer, (HW, PAGE_SIZE))
        + _page_specs(npages, layer, (HW, PAGE_SIZE)),
        out_specs=row,
    )
    return pl.pallas_call(
        functools.partial(_dsa_sample_kernel, npages=npages),
        grid_spec=gs,
        out_shape=jax.ShapeDtypeStruct((ns, 1, HW), F32),
        compiler_params=_cparams(("parallel",)),
    )(page_table, q, kn, vn, mask, *([ck] * npages), *([cv] * npages))


def _outproj_kernel(x_ref, fo_ref, do_ref, ho_ref, w_ref, g_ref, b_ref, o_ref):
    sub = (_dot(fo_ref[...].astype(BF16), w_ref[0:HW, :])
           + _dot(do_ref[...].astype(BF16), w_ref[HW:2 * HW, :])
           + _dot(ho_ref[...].astype(BF16), w_ref[2 * HW:, :]))
    o_ref[...] = _layer_norm(ALPHA * x_ref[...] + sub, g_ref[...], b_ref[...])


def outproj_norm(x_all, fo, do, ho, w_out, g, b, tm):
    n = x_all.shape[0]
    rowspec = lambda w: pl.BlockSpec((tm, w), lambda i: (i, 0))
    const = lambda shp: pl.BlockSpec(shp, lambda i: (0, 0))
    return pl.pallas_call(
        _outproj_kernel,
        grid=(n // tm,),
        in_specs=[rowspec(D_MODEL), rowspec(HW), rowspec(HW), rowspec(HGW),
                  const((D_MODEL, D_MODEL)), const((1, D_MODEL)), const((1, D_MODEL))],
        out_specs=rowspec(D_MODEL),
        out_shape=jax.ShapeDtypeStruct((n, D_MODEL), F32),
        compiler_params=_cparams(("parallel",)),
    )(x_all, fo, do, ho, w_out, g, b)


def _moe_kernel(x_ref, wrh_ref, wrl_ref, br_ref, w1_ref, w3_ref, w2_ref, g_ref, b_ref, o_ref,
                xb_scr, gate_scr, acc_scr):
    e = pl.program_id(1)
    tm = x_ref.shape[0]
    lane = _iota((tm, 128), 1)

    @pl.when(e == 0)
    def _():
        x = x_ref[...]
        xh = x.astype(BF16)
        xl = (x - xh.astype(F32)).astype(BF16)
        logits = (_dot(xh, wrh_ref[...]) + _dot(xl, wrh_ref[...]) + _dot(xh, wrl_ref[...])
                  + br_ref[...])
        gl = jnp.where(lane < N_GROUPS, logits, NEG_INF)
        gmax = jnp.max(gl, axis=-1, keepdims=True)
        p_top = 1.0 / jnp.sum(jnp.exp(gl - gmax), axis=-1, keepdims=True)
        lanef = lane.astype(F32)
        grp = jnp.min(jnp.where(gl == gmax, lanef, 1024.0), axis=-1, keepdims=True)
        lo = N_GROUPS + EPG * grp
        ev = jnp.where((lanef >= lo) & (lanef < lo + EPG), logits, NEG_INF)
        v1 = jnp.max(ev, axis=-1, keepdims=True)
        i1 = jnp.min(jnp.where(ev == v1, lanef, 1024.0), axis=-1, keepdims=True)
        ev2 = jnp.where(lanef == i1, NEG_INF, ev)
        v2 = jnp.max(ev2, axis=-1, keepdims=True)
        i2 = jnp.min(jnp.where(ev2 == v2, lanef, 1024.0), axis=-1, keepdims=True)
        t = jnp.exp(v2 - v1)
        g1 = p_top / (1.0 + t)
        g2 = p_top * t / (1.0 + t)
        gate_scr[...] = jnp.where(lanef == i1, g1, 0.0) + jnp.where(lanef == i2, g2, 0.0)
        xb_scr[...] = xh
        acc_scr[...] = jnp.zeros(acc_scr.shape, F32)

    xb = xb_scr[...]
    a = _dot(xb, w1_ref[0])
    h = (a / (1.0 + jnp.exp(-a))) * _dot(xb, w3_ref[0])
    y = _dot(h.astype(BF16), w2_ref[0])
    ge = jnp.sum(jnp.where(lane == e + N_GROUPS, gate_scr[...], 0.0), axis=-1, keepdims=True)
    acc_scr[...] += y * ge

    @pl.when(e == N_EXPERTS - 1)
    def _():
        o_ref[...] = _layer_norm(ALPHA * x_ref[...] + acc_scr[...], g_ref[...], b_ref[...])


def moe_norm(x_all, wr_hi, wr_lo, br, w1, w3, w2, g, b, tm):
    n = x_all.shape[0]
    const = lambda shp: pl.BlockSpec(shp, lambda i, e: (0, 0))
    return pl.pallas_call(
        _moe_kernel,
        grid=(n // tm, N_EXPERTS),
        in_specs=[pl.BlockSpec((tm, D_MODEL), lambda i, e: (i, 0)),
                  const((D_MODEL, 128)), const((D_MODEL, 128)), const((1, 128)),
                  pl.BlockSpec((1, D_MODEL, D_FF), lambda i, e: (e, 0, 0)),
                  pl.BlockSpec((1, D_MODEL, D_FF), lambda i, e: (e, 0, 0)),
                  pl.BlockSpec((1, D_FF, D_MODEL), lambda i, e: (e, 0, 0)),
                  const((1, D_MODEL)), const((1, D_MODEL))],
        out_specs=pl.BlockSpec((tm, D_MODEL), lambda i, e: (i, 0)),
        out_shape=jax.ShapeDtypeStruct((n, D_MODEL), F32),
        scratch_shapes=[pltpu.VMEM((tm, D_MODEL), BF16), pltpu.VMEM((tm, 128), F32),
                        pltpu.VMEM((tm, D_MODEL), F32)],
        compiler_params=_cparams(("parallel", "arbitrary")),
    )(x_all, wr_hi, wr_lo, br, w1, w3, w2, g, b)


def router_weights(wg, bg, we, be):
    w = jnp.concatenate([wg, we, jnp.zeros((D_MODEL, 128 - N_GROUPS - N_EXPERTS), F32)], -1)
    hi = w.astype(BF16)
    lo = (w - hi.astype(F32)).astype(BF16)
    br = jnp.concatenate([bg, be, jnp.zeros((128 - N_GROUPS - N_EXPERTS,), F32)])[None, :]
    return hi, lo, br


def forward(x_prompt, x_sample, cache_fox_k, cache_fox_v, cache_fox_logf, cache_dsa_k, cache_dsa_v,
            cache_idx_k, state_hgrn, page_table, w_in, b_fox, hg_lb, hg_norm, w_out, ln1_g, ln1_b,
            moe_wg, moe_bg, moe_we, moe_be, moe_w1, moe_w3, moe_w2, ln2_g, ln2_b,
            *, tm, tq_fox, tq_dsa, chunk, tm_moe):
    nb, seq, _ = x_prompt.shape
    ns = x_sample.shape[0]
    depth = w_in.shape[0]
    npr = nb * seq
    npages = page_table.shape[1]
    pool = cache_fox_k.shape[1]
    tps = seq // tm

    pad_rows = lambda a: jnp.concatenate([a, jnp.zeros((tm - ns, a.shape[-1]), F32)], 0)
    xp = x_prompt.reshape(npr, D_MODEL)
    xs = pad_rows(x_sample.reshape(ns, D_MODEL))
    w_p = permute_w_in(w_in)
    t64, t32, ts = rope_tables(seq, tm)
    lb, llb, l1m, om = lb_params(hg_lb)
    pages_t = lambda c: jnp.transpose(c, (0, 1, 3, 4, 2)).reshape(depth, pool, HW, PAGE_SIZE)
    ck_f, cv_f, ck_d, cv_d = (pages_t(c) for c in (cache_fox_k, cache_fox_v, cache_dsa_k,
                                                   cache_dsa_v))
    cik_t = jnp.swapaxes(cache_idx_k, 2, 3)
    clf_t = jnp.swapaxes(cache_fox_logf, 2, 3)
    state_t = jnp.transpose(state_hgrn, (0, 2, 3, 4, 1))
    w_out_b = w_out.astype(BF16)
    w1_b, w3_b, w2_b = moe_w1.astype(BF16), moe_w3.astype(BF16), moe_w2.astype(BF16)
    k_sel_s = min(TOPK_MAX, (npages * PAGE_SIZE + 1) // 4)

    outs_p = [[] for _ in range(7)]
    outs_s = [[] for _ in range(7)]
    for l in range(depth):
        bf_row = jnp.concatenate([b_fox[l], jnp.zeros((128 - H_FOX,), F32)])[None, :]
        pp, fkt, fvt, dkt, dvt, ikt, lft = project(xp, w_p[l], t64, t32, ts, bf_row, tm, nb, seq,
                                                   lambda i: i % tps)
        p_s, fkt_s, fvt_s, dkt_s, dvt_s, ikt_s, lft_s = project(
            xs, w_p[l], t64, t32, ts, bf_row, tm, 1, tm, lambda i: tps)
        ps = p_s[:ns]
        sm_s = ps[:, C_SM:C_SM + 128]

        fcum = cumsum_rows(lft.reshape(nb * 8, seq)).reshape(nb, 8, seq)
        fo_p = fox_prompt(pp, fkt, fvt, fcum, nb, seq, tq_fox)
        do_p = dsa_prompt(pp, dkt, dvt, ikt, nb, seq, tq_dsa)
        par = jnp.concatenate([llb[l:l + 1], l1m[l:l + 1], om[l:l + 1], hg_norm[l][None, :],
                               jnp.zeros((4, HGW), F32)], 0)
        ho_p, st_p = hgrn_prompt(pp, par, nb, seq, chunk)

        r3 = lambda a: a.reshape(ns, 1, a.shape[-1])
        lfn = jnp.concatenate([sm_s[:, 0:H_FOX], jnp.zeros((ns, 8 - H_FOX), F32)], -1)
        lfn_col = jnp.tile(lfn, (1, npages))[:, :, None]
        fo_s = fox_sample(page_table, r3(ps[:, C_FQ:C_FQ + HW]), r3(ps[:, C_FK:C_FK + HW]),
                          r3(ps[:, C_FV:C_FV + HW]), lfn_col, ck_f, cv_f, clf_t, l)
        iq_s = ps[:, C_IQ:C_IQ + HW].reshape(ns, N_IDX_HEADS, IDX_DIM)
        w_col = sm_s[:, SM_IW:SM_IW + N_IDX_HEADS][:, :, None]
        ik_s = sm_s[:, SM_IK:SM_IK + IDX_DIM]
        sc = dsa_sample_scores(page_table, iq_s, w_col, r3(ik_s), cik_t, l)
        mask = select_rows(sc.reshape(ns, -1), npages * PAGE_SIZE + 1, k_sel_s)
        do_s = dsa_sample(page_table, r3(ps[:, C_DQ:C_DQ + HW]), r3(ps[:, C_DK:C_DK + HW]),
                          r3(ps[:, C_DV:C_DV + HW]), mask[:, None, :], ck_d, cv_d, l)
        tr = lambda c: ps[:, c:c + HGW].T.reshape(H_HG, HG_D, ns)
        par_col = jnp.stack([llb[l], l1m[l], om[l]]).reshape(3, H_HG, HG_D, 1)
        ho_t, st_t = hgrn_sample(tr(C_HQ), tr(C_HF), tr(C_HI), tr(C_HG), state_t, l, par_col,
                                 hg_norm[l].reshape(H_HG, HG_D, 1))
        ho_s = ho_t.reshape(HGW, ns).T
        st_s = jnp.transpose(st_t, (3, 0, 1, 2))

        g1, b1 = ln1_g[l][None, :], ln1_b[l][None, :]
        g2, b2 = ln2_g[l][None, :], ln2_b[l][None, :]
        wr_hi, wr_lo, br = router_weights(moe_wg[l], moe_bg[l], moe_we[l], moe_be[l])
        xp = outproj_norm(xp, fo_p, do_p, ho_p, w_out_b[l], g1, b1, tm)
        xp = moe_norm(xp, wr_hi, wr_lo, br, w1_b[l], w3_b[l], w2_b[l], g2, b2, tm_moe)
        xs = outproj_norm(xs, pad_rows(fo_s.reshape(ns, HW)), pad_rows(do_s.reshape(ns, HW)),
                          pad_rows(ho_s), w_out_b[l], g1, b1, tm)
        xs = moe_norm(xs, wr_hi, wr_lo, br, w1_b[l], w3_b[l], w2_b[l], g2, b2, tm)

        hp = lambda a: jnp.transpose(a.reshape(nb, 4, HEAD_DIM, seq), (0, 3, 1, 2))
        for lst, a in zip(outs_p, (hp(fkt), hp(fvt), jnp.swapaxes(lft[:, 0:H_FOX], 1, 2),
                                   hp(dkt), hp(dvt), jnp.swapaxes(ikt, 1, 2), st_p)):
            lst.append(a)
        hs = lambda a: jnp.transpose(a[0, :, :ns].reshape(4, HEAD_DIM, ns), (2, 0, 1))[:, None]
        for lst, a in zip(outs_s, (hs(fkt_s), hs(fvt_s), lft_s[0, 0:H_FOX, :ns].T[:, None],
                                   hs(dkt_s), hs(dvt_s), ikt_s[0, :, :ns].T[:, None], st_s)):
            lst.append(a)

    y_p = xp.reshape(nb, seq, D_MODEL)
    y_s = xs[:ns].reshape(ns, 1, D_MODEL)
    return (y_p, y_s) + tuple(jnp.stack(a) for a in outs_p) + tuple(jnp.stack(a) for a in outs_s)


def kernel(x_prompt, x_sample, cache_fox_k, cache_fox_v, cache_fox_logf, cache_dsa_k, cache_dsa_v, cache_idx_k, state_hgrn, page_table, w_in, b_fox, hg_lb, hg_norm, w_out, ln1_g, ln1_b, moe_wg, moe_bg, moe_we, moe_be, moe_w1, moe_w3, moe_w2, ln2_g, ln2_b):
    return forward(x_prompt, x_sample, cache_fox_k, cache_fox_v, cache_fox_logf, cache_dsa_k,
                   cache_dsa_v, cache_idx_k, state_hgrn, page_table, w_in, b_fox, hg_lb, hg_norm,
                   w_out, ln1_g, ln1_b, moe_wg, moe_bg, moe_we, moe_be, moe_w1, moe_w3, moe_w2,
                   ln2_g, ln2_b, tm=256, tq_fox=256, tq_dsa=256, chunk=128, tm_moe=1024)
```

```python
import functools

import numpy as np
import jax
import jax.numpy as jnp
from jax import lax
from jax.experimental import pallas as pl
from jax.experimental.pallas import tpu as pltpu

F32 = jnp.float32
BF16 = jnp.bfloat16
I32 = jnp.int32

D_MODEL = 1024
DEPTH = 4
HEAD_DIM = 64
H_FOX = 4
H_DSA = 4
H_HG = 8
HG_D = 64
N_IDX_HEADS = 8
IDX_DIM = 32
PAGE_SIZE = 128
TOPK_MAX = 256
ROPE_THETA = 500000.0
N_GROUPS = 4
EPG = 4
N_EXPERTS = 16
D_FF = 512
LN_EPS = 1e-5
ALPHA = (2 * DEPTH) ** 0.25
HW = 256
HGW = 512

C_HQ, C_HF, C_HI, C_HG = 0, 512, 1024, 1536
C_FQ, C_FK, C_FV, C_DQ, C_DK, C_DV, C_IQ = (2048 + i * 256 for i in range(7))
C_SM = 3840
N_P = 3968
SM_IK = 32
SM_IW = 64

VMEM_LIMIT = 56 * 1024 * 1024
NEG_INF = float("-inf")


def _cparams(sem):
    return pltpu.CompilerParams(dimension_semantics=sem, vmem_limit_bytes=VMEM_LIMIT)


def _iota(shape, dim):
    return lax.broadcasted_iota(I32, shape, dim)


def _dot(a, b):
    return jnp.dot(a, b, preferred_element_type=F32)


def _dot_nt(a, b):
    return lax.dot_general(a, b, (((1,), (1,)), ((), ())), preferred_element_type=F32)


def _split3(x):
    h = x.astype(BF16)
    r = x - h.astype(F32)
    m = r.astype(BF16)
    l = (r - m.astype(F32)).astype(BF16)
    return h, m, l


def _dot3_l(x, w):
    h, m, l = _split3(x)
    return _dot(h, w) + _dot(m, w) + _dot(l, w)


def _dot3_r(w, x):
    h, m, l = _split3(x)
    return _dot(w, h) + _dot(w, m) + _dot(w, l)


def _layer_norm(y, g, b):
    mu = jnp.mean(y, axis=-1, keepdims=True)
    yc = y - mu
    var = jnp.mean(yc * yc, axis=-1, keepdims=True)
    return yc * lax.rsqrt(var + LN_EPS) * g + b


def _lb_kernel(x_ref, lb_ref, llb_ref, l1m_ref, om_ref):
    x = x_ref[...]
    m = jnp.max(x, axis=0, keepdims=True)
    e = jnp.exp(x - m)
    p = e / jnp.sum(e, axis=0, keepdims=True)
    n = x.shape[0]
    c = p[0:1]
    c0 = c
    for l in range(n):
        if l > 0:
            c = c + p[l:l + 1]
        lb = c - c0
        lb_ref[l:l + 1, :] = lb
        llb_ref[l:l + 1, :] = jnp.log(lb)
        l1m_ref[l:l + 1, :] = jnp.log1p(-lb)
        om_ref[l:l + 1, :] = 1.0 - lb


def lb_params(hg_lb):
    shp = jax.ShapeDtypeStruct(hg_lb.shape, F32)
    return pl.pallas_call(_lb_kernel, out_shape=(shp, shp, shp, shp))(hg_lb.astype(F32))


def _rope(v, t_ref, half):
    n = v.shape[-1]
    return (v * t_ref[0] + pltpu.roll(v, n - half, 1) * t_ref[1]
            + pltpu.roll(v, half, 1) * t_ref[2])


def _proj_kernel(x_ref, w_ref, t64_ref, t32_ref, ts_ref, bf_ref,
                 o_ref, fkt_ref, fvt_ref, dkt_ref, dvt_ref, ikt_ref, lft_ref):
    x = x_ref[...].astype(BF16)

    def seg(a, wd):
        return _dot(x, w_ref[:, a:a + wd])

    o_ref[:, C_HQ:C_HQ + 1024] = seg(C_HQ, 1024)
    o_ref[:, C_HI:C_HI + 1024] = seg(C_HI, 1024)
    o_ref[:, C_FQ:C_FQ + HW] = seg(C_FQ, HW)
    o_ref[:, C_DQ:C_DQ + HW] = _rope(seg(C_DQ, HW), t64_ref, 8)
    o_ref[:, C_IQ:C_IQ + HW] = _rope(seg(C_IQ, HW), t32_ref, 4)
    for c0, t_ref, rot in ((C_FK, fkt_ref, False), (C_FV, fvt_ref, False),
                           (C_DK, dkt_ref, True), (C_DV, dvt_ref, False)):
        v = seg(c0, HW)
        if rot:
            v = _rope(v, t64_ref, 8)
        o_ref[:, c0:c0 + HW] = v
        t_ref[0] = v.T
    sm = _rope(seg(C_SM, 128), ts_ref, 4)
    z = sm + bf_ref[...]
    logsig = jnp.minimum(z, 0.0) - jnp.log1p(jnp.exp(-jnp.abs(z)))
    sm = jnp.where(_iota(sm.shape, 1) < H_FOX, logsig, sm)
    o_ref[:, C_SM:C_SM + 128] = sm
    smt = sm.T
    lft_ref[0] = jnp.where(_iota((8, smt.shape[1]), 0) < H_FOX, smt[0:8], 0.0)
    ikt_ref[0] = smt[SM_IK:SM_IK + IDX_DIM]


def project(x, w_p, t64, t32, ts, bfox_row, tm, nb, seq, tab_tile):
    tps = seq // tm
    tab_map = lambda i: (0, tab_tile(i), 0)
    tspec = lambda w: pl.BlockSpec((1, w, tm), lambda i: (i // tps, 0, i % tps))
    tshape = lambda w: jax.ShapeDtypeStruct((nb, w, seq), F32)
    return pl.pallas_call(
        _proj_kernel,
        grid=(nb * tps,),
        in_specs=[
            pl.BlockSpec((tm, D_MODEL), lambda i: (i, 0)),
            pl.BlockSpec((D_MODEL, N_P), lambda i: (0, 0)),
            pl.BlockSpec((3, tm, HW), tab_map),
            pl.BlockSpec((3, tm, HW), tab_map),
            pl.BlockSpec((3, tm, 128), tab_map),
            pl.BlockSpec((1, 128), lambda i: (0, 0)),
        ],
        out_specs=[pl.BlockSpec((tm, N_P), lambda i: (i, 0)), tspec(HW), tspec(HW), tspec(HW),
                   tspec(HW), tspec(IDX_DIM), tspec(8)],
        out_shape=[jax.ShapeDtypeStruct((nb * seq, N_P), F32), tshape(HW), tshape(HW),
                   tshape(HW), tshape(HW), tshape(IDX_DIM), tshape(8)],
        compiler_params=_cparams(("parallel",)),
    )(x, w_p, t64, t32, ts, bfox_row)


def rope_tables(seq, tm):
    pos = jnp.concatenate([jnp.arange(seq), jnp.full((tm,), seq)]).astype(F32)

    def head_tabs(hd):
        rot = hd // 4
        half = rot // 2
        inv = ROPE_THETA ** (-jnp.arange(half, dtype=F32) * 2.0 / rot)
        ang = pos[:, None] * inv[None, :]
        cos, sin = jnp.cos(ang), jnp.sin(ang)
        n = pos.shape[0]
        one = jnp.ones((n, hd - rot), F32)
        zero = jnp.zeros((n, hd - rot), F32)
        zh = jnp.zeros((n, half), F32)
        c = jnp.concatenate([cos, cos, one], -1)
        sa = jnp.concatenate([-sin, zh, zero], -1)
        sb = jnp.concatenate([zh, sin, zero], -1)
        return c, sa, sb

    t64 = jnp.stack([jnp.tile(a, (1, HW // 64)) for a in head_tabs(64)])
    h32 = head_tabs(32)
    t32 = jnp.stack([jnp.tile(a, (1, HW // 32)) for a in h32])
    n = pos.shape[0]
    fill = [jnp.ones, jnp.zeros, jnp.zeros]
    ts = jnp.stack([jnp.concatenate([f((n, SM_IK), F32), a, f((n, 128 - SM_IK - 32), F32)], -1)
                    for f, a in zip(fill, h32)])
    return t64, t32, ts


def permute_w_in(w_in):
    L = w_in.shape[0]
    z = lambda n: jnp.zeros((L, D_MODEL, n), w_in.dtype)
    ik = w_in[:, :, 1796:1828]
    small = jnp.concatenate([w_in[:, :, 768:772], z(SM_IK - 4), ik, w_in[:, :, 1828:1836],
                             z(128 - SM_IW - 8)], -1)
    w = jnp.concatenate([w_in[:, :, 1836:3884], w_in[:, :, 0:768], w_in[:, :, 772:1540],
                         w_in[:, :, 1540:1796], small], -1)
    return w.astype(BF16)


def _cumsum_kernel(x_ref, tri_ref, o_ref):
    r, n = x_ref.shape
    carry = jnp.zeros((r, 1), F32)
    tri = tri_ref[...]
    for c in range(n // 128):
        blk = x_ref[:, c * 128:(c + 1) * 128]
        cs = _dot3_l(blk, tri) + carry
        o_ref[:, c * 128:(c + 1) * 128] = cs
        carry = carry + jnp.sum(blk, axis=-1, keepdims=True)


def cumsum_rows(x):
    tri = jnp.asarray(np.triu(np.ones((128, 128), np.float32)), BF16)
    return pl.pallas_call(_cumsum_kernel, out_shape=jax.ShapeDtypeStruct(x.shape, F32))(x, tri)


def _fox_prompt_kernel(q_ref, kt_ref, vt_ref, f_ref, o_ref, *, tq, qi):
    nvis = (qi + 1) * tq
    q = q_ref[...] * (HEAD_DIM ** -0.5)
    causal = _iota((tq, nvis), 1) <= _iota((tq, nvis), 0) + qi * tq
    outs = []
    for h in range(H_FOX):
        hs = slice(h * HEAD_DIM, (h + 1) * HEAD_DIM)
        s = _dot(q[:, hs].astype(BF16), kt_ref[0, hs, :].astype(BF16)) - f_ref[0, h:h + 1, :]
        s = jnp.where(causal, s, NEG_INF)
        p = jnp.exp(s - jnp.max(s, axis=-1, keepdims=True))
        l = jnp.sum(p, axis=-1, keepdims=True)
        outs.append(_dot_nt(p.astype(BF16), vt_ref[0, hs, :].astype(BF16)) / l)
    o_ref[0] = jnp.concatenate(outs, axis=-1)


def _stack_blocks(outs, nb, seq):
    return jnp.stack(outs, axis=1).reshape(nb * seq, outs[0].shape[-1])


def fox_prompt(p_all, fkt, fvt, fcum, nb, seq, tq):
    nq = seq // tq
    outs = []
    for qi in range(nq):
        nvis = (qi + 1) * tq
        kspec = lambda w: pl.BlockSpec((1, w, nvis), lambda b: (b, 0, 0))
        outs.append(pl.pallas_call(
            functools.partial(_fox_prompt_kernel, tq=tq, qi=qi),
            grid=(nb,),
            in_specs=[pl.BlockSpec((tq, HW), lambda b, qi=qi: (b * nq + qi, C_FQ // HW)),
                      kspec(HW), kspec(HW), kspec(8)],
            out_specs=pl.BlockSpec((1, tq, HW), lambda b: (b, 0, 0)),
            out_shape=jax.ShapeDtypeStruct((nb, tq, HW), F32),
            compiler_params=_cparams(("parallel",)),
        )(p_all, fkt, fvt, fcum))
    return _stack_blocks(outs, nb, seq)


def _sortable_key(score):
    bits = lax.bitcast_convert_type(score + 0.0, I32)
    return jnp.where(bits < 0, bits ^ jnp.int32(0x7FFFFFFF), bits)


def _kth_largest_key(key_ref, half_ref, k):
    rows, cols = key_ref.shape
    kf = float(k)
    i16 = jnp.int16
    one, zero = jnp.ones((), BF16), jnp.zeros((), BF16)

    def count(cmp, cand):
        cand16 = jnp.broadcast_to(cand, (rows, 128)).astype(i16)
        acc = jnp.zeros((rows, 128), BF16)
        for c in range(cols // 128):
            acc = acc + jnp.where(cmp(half_ref[:, c * 128:(c + 1) * 128], cand16), one, zero)
        return jnp.sum(acc.astype(F32), axis=-1, keepdims=True)

    def search(base):
        def body(i, t):
            cand = t + jnp.left_shift(jnp.int32(1), 15 - i)
            return jnp.where(base + count(lambda a, b: a >= b, cand) >= kf, cand, t)

        return lax.fori_loop(0, 16, body, jnp.full((rows, 1), -2 ** 15, I32))

    key = key_ref[...]
    hi = key >> 16
    half_ref[...] = hi.astype(i16)
    t_hi = search(0.0)
    above = count(lambda a, b: a > b, t_hi)
    half_ref[...] = jnp.where(hi == t_hi, (key & 0xFFFF) - 2 ** 15, -2 ** 15).astype(i16)
    t_lo = search(above)
    return t_hi * 65536 + (t_lo + 2 ** 15)


def _select_topk(key_ref, eq_ref, half_ref, vis, k):
    rows, cols = key_ref.shape
    kf = float(k)
    thr = _kth_largest_key(key_ref, half_ref, k)
    key = key_ref[...]
    gt = key > thr
    eqf = jnp.where(key == thr, 1.0, 0.0)
    eq_ref[...] = eqf
    need = kf - jnp.sum(jnp.where(gt, 1.0, 0.0), axis=-1, keepdims=True)
    n_eq = jnp.sum(eqf, axis=-1, keepdims=True)
    nbits = int(cols).bit_length()
    tied = (n_eq > need) & (thr > _sortable_key(jnp.full((1, 1), NEG_INF, F32)))
    any_tied = jnp.max(jnp.where(tied, 1.0, 0.0)) > 0.0

    def search():
        def body2(i, x):
            cand = x + jnp.left_shift(jnp.int32(1), nbits - 1 - i)
            col = _iota((rows, cols), 1)
            c = jnp.sum(jnp.where(col < cand, eq_ref[...], 0.0), axis=-1, keepdims=True)
            return jnp.where(c < need, cand, x)

        return lax.fori_loop(0, nbits, body2, jnp.zeros((rows, 1), I32))

    x = lax.cond(any_tied, search, lambda: jnp.full((rows, 1), cols, I32))
    x = jnp.where(tied, x, cols)
    col = _iota((rows, cols), 1)
    return vis & (gt | ((eq_ref[...] > 0.0) & (col <= x)))


def _dsa_prompt_kernel(dq_ref, iq_ref, sm_ref, dk_ref, dv_ref, ikt_ref, o_ref, key_ref, eq_ref,
                       half_ref, *, tq, k_sel, qi):
    nvis = (qi + 1) * tq
    iq = iq_ref[...] * (IDX_DIM ** -0.5)
    ikt = jnp.concatenate([ikt_ref[0].astype(BF16)] * N_IDX_HEADS, axis=0)
    sm = sm_ref[...]
    il = _iota(iq.shape, 1) >> 5
    score = jnp.zeros((tq, nvis), F32)
    for h in range(N_IDX_HEADS):
        rel = jnp.maximum(_dot(jnp.where(il == h, iq, 0.0).astype(BF16), ikt), 0.0)
        w = sm[:, SM_IW + h:SM_IW + h + 1] * (N_IDX_HEADS ** -0.5)
        score = score + w * rel
    vis = _iota((tq, nvis), 1) <= _iota((tq, nvis), 0) + qi * tq
    key_ref[...] = _sortable_key(jnp.where(vis, score, NEG_INF))
    sel = _select_topk(key_ref, eq_ref, half_ref, vis, k_sel)

    dq = dq_ref[...] * (HEAD_DIM ** -0.5)
    dk = dk_ref[0:nvis, :].astype(BF16)
    dv = dv_ref[0:nvis, :].astype(BF16)
    hl = _iota(dq.shape, 1) >> 6
    out = jnp.zeros((tq, HW), F32)
    for h in range(H_DSA):
        s = _dot_nt(jnp.where(hl == h, dq, 0.0).astype(BF16), dk)
        s = jnp.where(sel, s, NEG_INF)
        p = jnp.exp(s - jnp.max(s, axis=-1, keepdims=True))
        l = jnp.sum(p, axis=-1, keepdims=True)
        out = out + jnp.where(hl == h, _dot(p.astype(BF16), dv) / l, 0.0)
    o_ref[0] = out


def dsa_prompt(p_all, ikt, nb, seq, tq):
    nq = seq // tq
    k_sel = min(TOPK_MAX, seq // 4)
    outs = []
    for qi in range(nq):
        nvis = (qi + 1) * tq
        rows = nvis if seq % nvis == 0 else seq
        kspec = lambda c, rows=rows: pl.BlockSpec((rows, HW), lambda b: (b * (seq // rows), c))
        qspec = lambda c, w, qi=qi: pl.BlockSpec((tq, w), lambda b: (b * nq + qi, c))
        outs.append(pl.pallas_call(
            functools.partial(_dsa_prompt_kernel, tq=tq, k_sel=k_sel, qi=qi),
            grid=(nb,),
            in_specs=[qspec(C_DQ // HW, HW), qspec(C_IQ // HW, HW), qspec(C_SM // 128, 128),
                      kspec(C_DK // HW), kspec(C_DV // HW),
                      pl.BlockSpec((1, IDX_DIM, nvis), lambda b: (b, 0, 0))],
            out_specs=pl.BlockSpec((1, tq, HW), lambda b: (b, 0, 0)),
            out_shape=jax.ShapeDtypeStruct((nb, tq, HW), F32),
            scratch_shapes=[pltpu.VMEM((tq, nvis), I32), pltpu.VMEM((tq, nvis), F32),
                            pltpu.VMEM((tq, nvis), jnp.int16)],
            compiler_params=_cparams(("parallel",)),
        )(p_all, p_all, p_all, p_all, p_all, ikt))
    return _stack_blocks(outs, nb, seq)


def hgrn_level_mats(c):
    nl = int(np.log2(c))
    t = np.arange(c)
    mats = [(t[None, :] <= t[:, None]).astype(np.float32)]
    masks = []
    for L in range(nl):
        bit = (t >> L) & 1
        lo = (t >> L) << L
        w = np.zeros((c, c), np.float32)
        for r in range(c):
            if bit[r]:
                w[r, lo[r]:r + 1] = 1.0
            else:
                w[r, r + 1:lo[r] + (1 << L)] = 1.0
        mats.append(w)
        same = (t[:, None] >> (L + 1)) == (t[None, :] >> (L + 1))
        masks.append((bit[:, None] == 1) & (bit[None, :] == 0) & same)
    return np.stack(mats), np.stack(masks).astype(np.float32)


def _hgrn_gates(fl, llb, l1m, om):
    ls = jnp.minimum(fl, 0.0) - jnp.log1p(jnp.exp(-jnp.abs(fl)))
    b = l1m + ls
    logf = jnp.maximum(llb, b) + jnp.log1p(jnp.exp(-jnp.abs(llb - b)))
    kk = om / (1.0 + jnp.exp(fl))
    return logf, kk


def _hgrn_prompt_kernel(q_ref, f_ref, v_ref, g_ref, par_ref, wl_ref, ml_ref, bd_ref, o_ref, st_ref,
                        s_scr, *, c, nl):
    ci = pl.program_id(1)

    @pl.when(ci == 0)
    def _():
        s_scr[...] = jnp.zeros(s_scr.shape, F32)

    q = q_ref[...]
    v = v_ref[...]
    logf, kk = _hgrn_gates(f_ref[...], par_ref[0:1, :], par_ref[1:2, :], par_ref[2:3, :])
    lf3 = jnp.concatenate(_split3(logf), axis=-1)

    e_all = _dot(wl_ref[...], lf3)

    def rowsum(i):
        e = e_all[i * c:(i + 1) * c]
        return e[:, 0:HGW] + e[:, HGW:2 * HGW] + e[:, 2 * HGW:3 * HGW]

    bd = bd_ref[...]
    vb = v.astype(BF16)
    row = _iota((c, HGW), 0)
    hl = _iota((c, HW), 1) >> 6
    a_heads = [jnp.zeros((c, c), F32) for _ in range(H_HG)]
    for L in range(nl):
        e = rowsum(1 + L)
        x = (jnp.where(((row >> L) & 1) == 1, q, kk) * jnp.exp(e)).astype(BF16)
        msk = ml_ref[L] > 0.0
        for h in range(H_HG):
            xh = x[:, (h // 4) * HW:(h // 4 + 1) * HW]
            lhs = jnp.where(hl == (h % 4), xh, jnp.zeros_like(xh))
            a_heads[h] = a_heads[h] + jnp.where(msk, _dot_nt(lhs, xh), 0.0)
    o = _dot((q * kk).astype(BF16), bd) * v
    intra = []
    for half in range(2):
        acc = jnp.zeros((c, HW), F32)
        vh = vb[:, half * HW:(half + 1) * HW]
        for hh in range(4):
            acc = acc + jnp.where(hl == hh, _dot(a_heads[half * 4 + hh].astype(BF16), vh), 0.0)
        intra.append(acc)
    o = o + jnp.concatenate(intra, axis=-1)
    g = rowsum(0)
    st = s_scr[...]
    o = o + _dot_nt((q * jnp.exp(g)).astype(BF16), st.astype(BF16))
    gl = g[c - 1:c, :]
    kd = (kk * jnp.exp(gl - g)).astype(BF16)
    upd = _dot(v.T.astype(BF16), kd)
    s_new = st * jnp.exp(gl) + jnp.where(bd > 0, upd, 0.0)
    s_scr[...] = s_new
    st_ref[0] = s_new

    o2 = o * o
    o2h = o2.astype(BF16)
    o2l = (o2 - o2h.astype(F32)).astype(BF16)
    ms = (_dot(o2h, bd) + _dot(o2l, bd)) * (1.0 / HG_D)
    hg = g_ref[...]
    o_ref[...] = o * lax.rsqrt(ms + LN_EPS) * par_ref[3:4, :] * (hg / (1.0 + jnp.exp(-hg)))


def hgrn_prompt(p_all, par, nb, seq, c):
    nc = seq // c
    nl = int(np.log2(c))
    wl, ml = hgrn_level_mats(c)
    bd = np.kron(np.eye(H_HG, dtype=np.float32), np.ones((HG_D, HG_D), np.float32))
    o, st = pl.pallas_call(
        functools.partial(_hgrn_prompt_kernel, c=c, nl=nl),
        grid=(nb, nc),
        in_specs=[
            pl.BlockSpec((c, HGW), lambda b, i: (b * nc + i, C_HQ // HGW)),
            pl.BlockSpec((c, HGW), lambda b, i: (b * nc + i, C_HF // HGW)),
            pl.BlockSpec((c, HGW), lambda b, i: (b * nc + i, C_HI // HGW)),
            pl.BlockSpec((c, HGW), lambda b, i: (b * nc + i, C_HG // HGW)),
            pl.BlockSpec((8, HGW), lambda b, i: (0, 0)),
            pl.BlockSpec(((nl + 1) * c, c), lambda b, i: (0, 0)),
            pl.BlockSpec((nl, c, c), lambda b, i: (0, 0, 0)),
            pl.BlockSpec((HGW, HGW), lambda b, i: (0, 0)),
        ],
        out_specs=[
            pl.BlockSpec((c, HGW), lambda b, i: (b * nc + i, 0)),
            pl.BlockSpec((1, HGW, HGW), lambda b, i: (b, 0, 0)),
        ],
        out_shape=[jax.ShapeDtypeStruct((nb * seq, HGW), F32),
                   jax.ShapeDtypeStruct((nb, HGW, HGW), F32)],
        scratch_shapes=[pltpu.VMEM((HGW, HGW), F32)],
        compiler_params=_cparams(("parallel", "arbitrary")),
    )(p_all, p_all, p_all, p_all, par, jnp.asarray(wl.reshape(-1, c), BF16), jnp.asarray(ml, F32),
      jnp.asarray(bd, BF16))
    ar = jnp.arange(H_HG)
    st = st.reshape(nb, H_HG, HG_D, H_HG, HG_D)[:, ar, :, ar, :]
    return o, jnp.transpose(st, (1, 0, 3, 2))


def _hgrn_sample_kernel(q_ref, f_ref, v_ref, g_ref, s_ref, pc_ref, n_ref, o_ref, so_ref,
                        f_scr, k_scr):
    logf, kk = _hgrn_gates(f_ref[0], pc_ref[0, 0], pc_ref[1, 0], pc_ref[2, 0])
    f_scr[...] = jnp.exp(logf)
    k_scr[...] = kk
    v = v_ref[0]

    def body(d, o):
        sn = f_scr[pl.ds(d, 1), :] * s_ref[0, 0, d] + k_scr[pl.ds(d, 1), :] * v
        so_ref[0, d] = sn
        return o + q_ref[0, pl.ds(d, 1), :] * sn

    o = lax.fori_loop(0, HG_D, body, jnp.zeros(v.shape, F32), unroll=8)
    ms = jnp.mean(o * o, axis=0, keepdims=True)
    hg = g_ref[0]
    o_ref[0] = o * lax.rsqrt(ms + LN_EPS) * n_ref[0] * (hg / (1.0 + jnp.exp(-hg)))


def hgrn_sample(qt, ft, vt, gt, state_t, layer, par_col, norm_col):
    ns = qt.shape[-1]
    vec = pl.BlockSpec((1, HG_D, ns), lambda h: (h, 0, 0))
    return pl.pallas_call(
        _hgrn_sample_kernel,
        grid=(H_HG,),
        in_specs=[vec, vec, vec, vec,
                  pl.BlockSpec((1, 1, HG_D, HG_D, ns), lambda h: (layer, h, 0, 0, 0)),
                  pl.BlockSpec((3, 1, HG_D, 1), lambda h: (0, h, 0, 0)),
                  pl.BlockSpec((1, HG_D, 1), lambda h: (h, 0, 0))],
        out_specs=[vec, pl.BlockSpec((1, HG_D, HG_D, ns), lambda h: (h, 0, 0, 0))],
        out_shape=[jax.ShapeDtypeStruct((H_HG, HG_D, ns), F32),
                   jax.ShapeDtypeStruct((H_HG, HG_D, HG_D, ns), F32)],
        scratch_shapes=[pltpu.VMEM((HG_D, ns), F32), pltpu.VMEM((HG_D, ns), F32)],
        compiler_params=_cparams(("parallel",)),
    )(qt, ft, vt, gt, state_t, par_col, norm_col)


def _head_rows(q, scale):
    qb = jnp.broadcast_to(q * scale, (8, HW))
    return jnp.where((_iota((8, HW), 1) >> 6) == _iota((8, HW), 0), qb, 0.0)


def _merge_heads(acc):
    return jnp.sum(jnp.where((_iota((8, HW), 1) >> 6) == _iota((8, HW), 0), acc, 0.0),
                   axis=0, keepdims=True)


def _fox_sample_kernel(pt_ref, q_ref, kn_ref, vn_ref, lfn_ref, u_ref, mc_ref, *rest, npages):
    k_refs = rest[0:npages]
    v_refs = rest[npages:2 * npages]
    lf_refs = rest[2 * npages:3 * npages]
    o_ref = rest[3 * npages]
    lf_scr = rest[3 * npages + 1]
    q8 = _head_rows(q_ref[0], HEAD_DIM ** -0.5)
    q8b = q8.astype(BF16)
    lf_scr[...] = jnp.zeros(lf_scr.shape, F32)
    for p in range(npages):
        lf_scr[8 * p:8 * p + H_FOX, :] = lf_refs[p][0, 0]
    lf = lf_scr[...]
    tot = jnp.broadcast_to(jnp.sum(lf, axis=-1, keepdims=True), lf.shape)
    bias = _dot3_l(lf, u_ref[...]) + _dot3_r(mc_ref[...], tot) + lfn_ref[0]
    s_new = jnp.sum(q8 * kn_ref[0], axis=-1, keepdims=True)
    ss = []
    m = s_new
    for p in range(npages):
        s = _dot(q8b, k_refs[p][0, 0].astype(BF16)) + bias[8 * p:8 * p + 8, :]
        ss.append(s)
        m = jnp.maximum(m, jnp.max(s, axis=-1, keepdims=True))
    pn = jnp.exp(s_new - m)
    l = pn
    acc = pn * vn_ref[0]
    for p in range(npages):
        e = jnp.exp(ss[p] - m)
        l = l + jnp.sum(e, axis=-1, keepdims=True)
        acc = acc + _dot_nt(e.astype(BF16), v_refs[p][0, 0].astype(BF16))
    o_ref[0] = _merge_heads(acc / l)


def _page_specs(npages, layer, blk):
    def mk(p):
        return pl.BlockSpec((1, 1) + blk, lambda b, pt: (layer, pt[b, p], 0, 0))
    return [mk(p) for p in range(npages)]


def fox_sample(page_table, q, kn, vn, lfn_col, ck, cv, clf_t, layer):
    ns, npages = page_table.shape
    r = 8 * npages
    u = np.tril(np.ones((PAGE_SIZE, PAGE_SIZE), np.float32), -1)
    pg = np.arange(r) // 8
    hd = np.arange(r) % 8
    mc = ((hd[:, None] == hd[None, :]) & (pg[None, :] > pg[:, None])).astype(np.float32)
    row = pl.BlockSpec((1, 1, HW), lambda b, pt: (b, 0, 0))
    gs = pltpu.PrefetchScalarGridSpec(
        num_scalar_prefetch=1,
        grid=(ns,),
        in_specs=[row, row, row,
                  pl.BlockSpec((1, r, 1), lambda b, pt: (b, 0, 0)),
                  pl.BlockSpec((PAGE_SIZE, PAGE_SIZE), lambda b, pt: (0, 0)),
                  pl.BlockSpec((r, r), lambda b, pt: (0, 0))]
        + _page_specs(npages, layer, (HW, PAGE_SIZE))
        + _page_specs(npages, layer, (HW, PAGE_SIZE))
        + _page_specs(npages, layer, (H_FOX, PAGE_SIZE)),
        out_specs=row,
        scratch_shapes=[pltpu.VMEM((r, PAGE_SIZE), F32)],
    )
    return pl.pallas_call(
        functools.partial(_fox_sample_kernel, npages=npages),
        grid_spec=gs,
        out_shape=jax.ShapeDtypeStruct((ns, 1, HW), F32),
        compiler_params=_cparams(("parallel",)),
    )(page_table, q, kn, vn, lfn_col, jnp.asarray(u, BF16), jnp.asarray(mc, BF16),
      *([ck] * npages), *([cv] * npages), *([clf_t] * npages))


def _dsa_scores_kernel(pt_ref, iq_ref, w_ref, ikn_ref, *rest, npages):
    ik_refs = rest[0:npages]
    o_ref = rest[npages]
    iq = iq_ref[0] * (IDX_DIM ** -0.5)
    w = w_ref[0] * (N_IDX_HEADS ** -0.5)
    iqb = iq.astype(BF16)
    for p in range(npages):
        rel = jnp.maximum(_dot(iqb, ik_refs[p][0, 0].astype(BF16)), 0.0)
        o_ref[0, :, p * PAGE_SIZE:(p + 1) * PAGE_SIZE] = jnp.sum(w * rel, axis=0, keepdims=True)
    rel_n = jnp.maximum(jnp.sum(iq * ikn_ref[0], axis=-1, keepdims=True), 0.0)
    sc_n = jnp.sum(w * rel_n, axis=0, keepdims=True)
    past = npages * PAGE_SIZE
    o_ref[0, :, past:past + PAGE_SIZE] = jnp.where(_iota((1, PAGE_SIZE), 1) == 0, sc_n, NEG_INF)


def dsa_sample_scores(page_table, iq, w_col, ikn, cik, layer):
    ns, npages = page_table.shape
    ncol = (npages + 1) * PAGE_SIZE
    gs = pltpu.PrefetchScalarGridSpec(
        num_scalar_prefetch=1,
        grid=(ns,),
        in_specs=[pl.BlockSpec((1, N_IDX_HEADS, IDX_DIM), lambda b, pt: (b, 0, 0)),
                  pl.BlockSpec((1, N_IDX_HEADS, 1), lambda b, pt: (b, 0, 0)),
                  pl.BlockSpec((1, 1, IDX_DIM), lambda b, pt: (b, 0, 0))]
        + _page_specs(npages, layer, (IDX_DIM, PAGE_SIZE)),
        out_specs=pl.BlockSpec((1, 1, ncol), lambda b, pt: (b, 0, 0)),
    )
    return pl.pallas_call(
        functools.partial(_dsa_scores_kernel, npages=npages),
        grid_spec=gs,
        out_shape=jax.ShapeDtypeStruct((ns, 1, ncol), F32),
        compiler_params=_cparams(("parallel",)),
    )(page_table, iq, w_col, ikn, *([cik] * npages))


def _select_kernel(s_ref, o_ref, key_ref, eq_ref, half_ref, *, n_vis, k_sel):
    vis = _iota(s_ref.shape, 1) < n_vis
    key_ref[...] = _sortable_key(jnp.where(vis, s_ref[...], NEG_INF))
    sel = _select_topk(key_ref, eq_ref, half_ref, vis, k_sel)
    o_ref[...] = jnp.where(sel, 1.0, 0.0)


def select_rows(scores, n_vis, k_sel):
    return pl.pallas_call(
        functools.partial(_select_kernel, n_vis=n_vis, k_sel=k_sel),
        out_shape=jax.ShapeDtypeStruct(scores.shape, F32),
        scratch_shapes=[pltpu.VMEM(scores.shape, I32), pltpu.VMEM(scores.shape, F32),
                        pltpu.VMEM(scores.shape, jnp.int16)],
        compiler_params=pltpu.CompilerParams(vmem_limit_bytes=VMEM_LIMIT),
    )(scores)


def _dsa_sample_kernel(pt_ref, q_ref, kn_ref, vn_ref, mk_ref, *rest, npages):
    k_refs = rest[0:npages]
    v_refs = rest[npages:2 * npages]
    o_ref = rest[2 * npages]
    q8 = _head_rows(q_ref[0], HEAD_DIM ** -0.5)
    q8b = q8.astype(BF16)
    past = npages * PAGE_SIZE
    s_new = jnp.sum(q8 * kn_ref[0], axis=-1, keepdims=True)
    s_new = jnp.where(mk_ref[0, :, past:past + 1] > 0.0, s_new, NEG_INF)
    ss = []
    m = s_new
    for p in range(npages):
        s = _dot(q8b, k_refs[p][0, 0].astype(BF16))
        s = jnp.where(mk_ref[0, :, p * PAGE_SIZE:(p + 1) * PAGE_SIZE] > 0.0, s, NEG_INF)
        ss.append(s)
        m = jnp.maximum(m, jnp.max(s, axis=-1, keepdims=True))
    pn = jnp.exp(s_new - m)
    l = pn
    acc = pn * vn_ref[0]
    for p in range(npages):
        e = jnp.exp(ss[p] - m)
        l = l + jnp.sum(e, axis=-1, keepdims=True)
        acc = acc + _dot_nt(e.astype(BF16), v_refs[p][0, 0].astype(BF16))
    o_ref[0] = _merge_heads(acc / l)


def dsa_sample(page_table, q, kn, vn, mask, ck, cv, layer):
    ns, npages = page_table.shape
    ncol = mask.shape[-1]
    row = pl.BlockSpec((1, 1, HW), lambda b, pt: (b, 0, 0))
    gs = pltpu.PrefetchScalarGridSpec(
        num_scalar_prefetch=1,
        grid=(ns,),
        in_specs=[row, row, row, pl.BlockSpec((1, 1, ncol), lambda b, pt: (b, 0, 0))]
        + _page_specs(npages, layer, (HW, PAGE_SIZE))
        + _page_specs(npages, layer, (HW, PAGE_SIZE)),
        out_specs=row,
    )
    return pl.pallas_call(
        functools.partial(_dsa_sample_kernel, npages=npages),
        grid_spec=gs,
        out_shape=jax.ShapeDtypeStruct((ns, 1, HW), F32),
        compiler_params=_cparams(("parallel",)),
    )(page_table, q, kn, vn, mask, *([ck] * npages), *([cv] * npages))


def _outproj_kernel(x_ref, fo_ref, do_ref, ho_ref, w_ref, g_ref, b_ref, o_ref):
    sub = (_dot(fo_ref[...].astype(BF16), w_ref[0:HW, :])
           + _dot(do_ref[...].astype(BF16), w_ref[HW:2 * HW, :])
           + _dot(ho_ref[...].astype(BF16), w_ref[2 * HW:, :]))
    o_ref[...] = _layer_norm(ALPHA * x_ref[...] + sub, g_ref[...], b_ref[...])


def outproj_norm(x_all, fo, do, ho, w_out, g, b, tm):
    n = x_all.shape[0]
    rowspec = lambda w: pl.BlockSpec((tm, w), lambda i: (i, 0))
    const = lambda shp: pl.BlockSpec(shp, lambda i: (0, 0))
    return pl.pallas_call(
        _outproj_kernel,
        grid=(n // tm,),
        in_specs=[rowspec(D_MODEL), rowspec(HW), rowspec(HW), rowspec(HGW),
                  const((D_MODEL, D_MODEL)), const((1, D_MODEL)), const((1, D_MODEL))],
        out_specs=rowspec(D_MODEL),
        out_shape=jax.ShapeDtypeStruct((n, D_MODEL), F32),
        compiler_params=_cparams(("parallel",)),
    )(x_all, fo, do, ho, w_out, g, b)


def _moe_kernel(x_ref, wrh_ref, wrl_ref, br_ref, tri_ref, w1_ref, w3_ref, w2_ref, g_ref, b_ref,
                o_ref, xs_scr, gate_scr, acc_scr, pos_scr, bnd_ref, *, rb):
    e = pl.program_id(1)
    tm = x_ref.shape[0]
    lane = _iota((tm, 128), 1)

    @pl.when(e == 0)
    def _():
        x = x_ref[...]
        xh = x.astype(BF16)
        xl = (x - xh.astype(F32)).astype(BF16)
        logits = (_dot(xh, wrh_ref[...]) + _dot(xl, wrh_ref[...]) + _dot(xh, wrl_ref[...])
                  + br_ref[...])
        gl = jnp.where(lane < N_GROUPS, logits, NEG_INF)
        gmax = jnp.max(gl, axis=-1, keepdims=True)
        p_top = 1.0 / jnp.sum(jnp.exp(gl - gmax), axis=-1, keepdims=True)
        lanef = lane.astype(F32)
        grp = jnp.min(jnp.where(gl == gmax, lanef, 1024.0), axis=-1, keepdims=True)
        lo = N_GROUPS + EPG * grp
        ev = jnp.where((lanef >= lo) & (lanef < lo + EPG), logits, NEG_INF)
        v1 = jnp.max(ev, axis=-1, keepdims=True)
        i1 = jnp.min(jnp.where(ev == v1, lanef, 1024.0), axis=-1, keepdims=True)
        ev2 = jnp.where(lanef == i1, NEG_INF, ev)
        v2 = jnp.max(ev2, axis=-1, keepdims=True)
        i2 = jnp.min(jnp.where(ev2 == v2, lanef, 1024.0), axis=-1, keepdims=True)
        t = jnp.exp(v2 - v1)
        g1 = p_top / (1.0 + t)
        g2 = p_top * t / (1.0 + t)
        gates = jnp.where(lanef == i1, g1, 0.0) + jnp.where(lanef == i2, g2, 0.0)

        onehot = jnp.where(lanef == grp, 1.0, 0.0)
        earlier = _dot(tri_ref[...], onehot.astype(BF16))
        n = jnp.sum(onehot, axis=0, keepdims=True)
        upper = jnp.where(_iota((128, 128), 0) < _iota((128, 128), 1), 1.0, 0.0).astype(BF16)
        start = _dot3_l(jnp.broadcast_to(n, (8, 128)), upper)[0:1]
        pos = jnp.sum(onehot * (start + earlier), axis=-1, keepdims=True)
        pos_scr[...] = jnp.broadcast_to(pos, (tm, 128))
        pos_row = pos_scr[...].T[0:1, :]
        perm = jnp.where(_iota((tm, tm), 0).astype(F32) == pos_row, 1.0, 0.0).astype(BF16)
        xs_scr[...] = _dot(perm, xh).astype(BF16)
        gate_scr[...] = _dot3_r(perm, gates)
        acc_scr[...] = jnp.zeros(acc_scr.shape, F32)
        lane1 = _iota((1, 128), 1)
        for g in range(N_GROUPS):
            s_g = jnp.sum(jnp.where(lane1 == g, start, 0.0))
            n_g = jnp.sum(jnp.where(lane1 == g, n, 0.0))
            bnd_ref[g] = s_g.astype(I32)
            bnd_ref[N_GROUPS + g] = (s_g + n_g).astype(I32)

    grp_e = e // EPG
    first, last = bnd_ref[grp_e], bnd_ref[N_GROUPS + grp_e]
    for j in range(tm // rb):
        @pl.when((first < (j + 1) * rb) & (last > j * rb))
        def _():
            rows = slice(j * rb, (j + 1) * rb)
            xb = xs_scr[rows, :]
            a = _dot(xb, w1_ref[0])
            h = (a / (1.0 + jnp.exp(-a))) * _dot(xb, w3_ref[0])
            y = _dot(h.astype(BF16), w2_ref[0])
            ge = jnp.sum(jnp.where(_iota((rb, 128), 1) == e + N_GROUPS, gate_scr[rows, :], 0.0),
                         axis=-1, keepdims=True)
            acc_scr[rows, :] += y * ge

    @pl.when(e == N_EXPERTS - 1)
    def _():
        unperm = jnp.where(_iota((tm, tm), 1).astype(F32) == pos_scr[:, 0:1], 1.0, 0.0).astype(BF16)
        y = _dot3_r(unperm, acc_scr[...])
        o_ref[...] = _layer_norm(ALPHA * x_ref[...] + y, g_ref[...], b_ref[...])


def moe_norm(x_all, wr_hi, wr_lo, br, w1, w3, w2, g, b, tm, rb):
    n = x_all.shape[0]
    const = lambda shp: pl.BlockSpec(shp, lambda i, e: (0, 0))
    tri = jnp.asarray(np.tril(np.ones((tm, tm), np.float32), -1), BF16)
    return pl.pallas_call(
        functools.partial(_moe_kernel, rb=rb),
        grid=(n // tm, N_EXPERTS),
        in_specs=[pl.BlockSpec((tm, D_MODEL), lambda i, e: (i, 0)),
                  const((D_MODEL, 128)), const((D_MODEL, 128)), const((1, 128)), const((tm, tm)),
                  pl.BlockSpec((1, D_MODEL, D_FF), lambda i, e: (e, 0, 0)),
                  pl.BlockSpec((1, D_MODEL, D_FF), lambda i, e: (e, 0, 0)),
                  pl.BlockSpec((1, D_FF, D_MODEL), lambda i, e: (e, 0, 0)),
                  const((1, D_MODEL)), const((1, D_MODEL))],
        out_specs=pl.BlockSpec((tm, D_MODEL), lambda i, e: (i, 0)),
        out_shape=jax.ShapeDtypeStruct((n, D_MODEL), F32),
        scratch_shapes=[pltpu.VMEM((tm, D_MODEL), BF16), pltpu.VMEM((tm, 128), F32),
                        pltpu.VMEM((tm, D_MODEL), F32), pltpu.VMEM((tm, 128), F32),
                        pltpu.SMEM((2 * N_GROUPS,), I32)],
        compiler_params=_cparams(("parallel", "arbitrary")),
    )(x_all, wr_hi, wr_lo, br, tri, w1, w3, w2, g, b)


def router_weights(wg, bg, we, be):
    w = jnp.concatenate([wg, we, jnp.zeros((D_MODEL, 128 - N_GROUPS - N_EXPERTS), F32)], -1)
    hi = w.astype(BF16)
    lo = (w - hi.astype(F32)).astype(BF16)
    br = jnp.concatenate([bg, be, jnp.zeros((128 - N_GROUPS - N_EXPERTS,), F32)])[None, :]
    return hi, lo, br


def forward(x_prompt, x_sample, cache_fox_k, cache_fox_v, cache_fox_logf, cache_dsa_k, cache_dsa_v,
            cache_idx_k, state_hgrn, page_table, w_in, b_fox, hg_lb, hg_norm, w_out, ln1_g, ln1_b,
            moe_wg, moe_bg, moe_we, moe_be, moe_w1, moe_w3, moe_w2, ln2_g, ln2_b,
            *, tm, tq_fox, tq_dsa, chunk, tm_moe, rb_moe):
    nb, seq, _ = x_prompt.shape
    ns = x_sample.shape[0]
    depth = w_in.shape[0]
    npr = nb * seq
    npages = page_table.shape[1]
    pool = cache_fox_k.shape[1]
    tps = seq // tm

    pad_rows = lambda a: jnp.concatenate([a, jnp.zeros((tm - ns, a.shape[-1]), F32)], 0)
    xp = x_prompt.reshape(npr, D_MODEL)
    xs = pad_rows(x_sample.reshape(ns, D_MODEL))
    w_p = permute_w_in(w_in)
    t64, t32, ts = rope_tables(seq, tm)
    lb, llb, l1m, om = lb_params(hg_lb)
    pages_t = lambda c: jnp.transpose(c, (0, 1, 3, 4, 2)).reshape(depth, pool, HW, PAGE_SIZE)
    ck_f, cv_f, ck_d, cv_d = (pages_t(c) for c in (cache_fox_k, cache_fox_v, cache_dsa_k,
                                                   cache_dsa_v))
    cik_t = jnp.swapaxes(cache_idx_k, 2, 3)
    clf_t = jnp.swapaxes(cache_fox_logf, 2, 3)
    state_t = jnp.transpose(state_hgrn, (0, 2, 3, 4, 1))
    w_out_b = w_out.astype(BF16)
    w1_b, w3_b, w2_b = moe_w1.astype(BF16), moe_w3.astype(BF16), moe_w2.astype(BF16)
    k_sel_s = min(TOPK_MAX, (npages * PAGE_SIZE + 1) // 4)

    outs_p = [[] for _ in range(7)]
    outs_s = [[] for _ in range(7)]
    for l in range(depth):
        bf_row = jnp.concatenate([b_fox[l], jnp.zeros((128 - H_FOX,), F32)])[None, :]
        pp, fkt, fvt, dkt, dvt, ikt, lft = project(xp, w_p[l], t64, t32, ts, bf_row, tm, nb, seq,
                                                   lambda i: i % tps)
        p_s, fkt_s, fvt_s, dkt_s, dvt_s, ikt_s, lft_s = project(
            xs, w_p[l], t64, t32, ts, bf_row, tm, 1, tm, lambda i: tps)
        ps = p_s[:ns]
        sm_s = ps[:, C_SM:C_SM + 128]

        fcum = cumsum_rows(lft.reshape(nb * 8, seq)).reshape(nb, 8, seq)
        fo_p = fox_prompt(pp, fkt, fvt, fcum, nb, seq, tq_fox)
        do_p = dsa_prompt(pp, ikt, nb, seq, tq_dsa)
        par = jnp.concatenate([llb[l:l + 1], l1m[l:l + 1], om[l:l + 1], hg_norm[l][None, :],
                               jnp.zeros((4, HGW), F32)], 0)
        ho_p, st_p = hgrn_prompt(pp, par, nb, seq, chunk)

        r3 = lambda a: a.reshape(ns, 1, a.shape[-1])
        lfn = jnp.concatenate([sm_s[:, 0:H_FOX], jnp.zeros((ns, 8 - H_FOX), F32)], -1)
        lfn_col = jnp.tile(lfn, (1, npages))[:, :, None]
        fo_s = fox_sample(page_table, r3(ps[:, C_FQ:C_FQ + HW]), r3(ps[:, C_FK:C_FK + HW]),
                          r3(ps[:, C_FV:C_FV + HW]), lfn_col, ck_f, cv_f, clf_t, l)
        iq_s = ps[:, C_IQ:C_IQ + HW].reshape(ns, N_IDX_HEADS, IDX_DIM)
        w_col = sm_s[:, SM_IW:SM_IW + N_IDX_HEADS][:, :, None]
        ik_s = sm_s[:, SM_IK:SM_IK + IDX_DIM]
        sc = dsa_sample_scores(page_table, iq_s, w_col, r3(ik_s), cik_t, l)
        mask = select_rows(sc.reshape(ns, -1), npages * PAGE_SIZE + 1, k_sel_s)
        do_s = dsa_sample(page_table, r3(ps[:, C_DQ:C_DQ + HW]), r3(ps[:, C_DK:C_DK + HW]),
                          r3(ps[:, C_DV:C_DV + HW]), mask[:, None, :], ck_d, cv_d, l)
        tr = lambda c: ps[:, c:c + HGW].T.reshape(H_HG, HG_D, ns)
        par_col = jnp.stack([llb[l], l1m[l], om[l]]).reshape(3, H_HG, HG_D, 1)
        ho_t, st_t = hgrn_sample(tr(C_HQ), tr(C_HF), tr(C_HI), tr(C_HG), state_t, l, par_col,
                                 hg_norm[l].reshape(H_HG, HG_D, 1))
        ho_s = ho_t.reshape(HGW, ns).T
        st_s = jnp.transpose(st_t, (3, 0, 1, 2))

        g1, b1 = ln1_g[l][None, :], ln1_b[l][None, :]
        g2, b2 = ln2_g[l][None, :], ln2_b[l][None, :]
        wr_hi, wr_lo, br = router_weights(moe_wg[l], moe_bg[l], moe_we[l], moe_be[l])
        xp = outproj_norm(xp, fo_p, do_p, ho_p, w_out_b[l], g1, b1, tm)
        xp = moe_norm(xp, wr_hi, wr_lo, br, w1_b[l], w3_b[l], w2_b[l], g2, b2, tm_moe, rb_moe)
        xs = outproj_norm(xs, pad_rows(fo_s.reshape(ns, HW)), pad_rows(do_s.reshape(ns, HW)),
                          pad_rows(ho_s), w_out_b[l], g1, b1, tm)
        xs = moe_norm(xs, wr_hi, wr_lo, br, w1_b[l], w3_b[l], w2_b[l], g2, b2, tm, tm)

        hp = lambda a: jnp.transpose(a.reshape(nb, 4, HEAD_DIM, seq), (0, 3, 1, 2))
        for lst, a in zip(outs_p, (hp(fkt), hp(fvt), jnp.swapaxes(lft[:, 0:H_FOX], 1, 2),
                                   hp(dkt), hp(dvt), jnp.swapaxes(ikt, 1, 2), st_p)):
            lst.append(a)
        hs = lambda a: jnp.transpose(a[0, :, :ns].reshape(4, HEAD_DIM, ns), (2, 0, 1))[:, None]
        for lst, a in zip(outs_s, (hs(fkt_s), hs(fvt_s), lft_s[0, 0:H_FOX, :ns].T[:, None],
                                   hs(dkt_s), hs(dvt_s), ikt_s[0, :, :ns].T[:, None], st_s)):
            lst.append(a)

    y_p = xp.reshape(nb, seq, D_MODEL)
    y_s = xs[:ns].reshape(ns, 1, D_MODEL)
    return (y_p, y_s) + tuple(jnp.stack(a) for a in outs_p) + tuple(jnp.stack(a) for a in outs_s)


def kernel(x_prompt, x_sample, cache_fox_k, cache_fox_v, cache_fox_logf, cache_dsa_k, cache_dsa_v, cache_idx_k, state_hgrn, page_table, w_in, b_fox, hg_lb, hg_norm, w_out, ln1_g, ln1_b, moe_wg, moe_bg, moe_we, moe_be, moe_w1, moe_w3, moe_w2, ln2_g, ln2_b):
    return forward(x_prompt, x_sample, cache_fox_k, cache_fox_v, cache_fox_logf, cache_dsa_k,
                   cache_dsa_v, cache_idx_k, state_hgrn, page_table, w_in, b_fox, hg_lb, hg_norm,
                   w_out, ln1_g, ln1_b, moe_wg, moe_bg, moe_we, moe_be, moe_w1, moe_w3, moe_w2,
                   ln2_g, ln2_b, tm=256, tq_fox=256, tq_dsa=256, chunk=128, tm_moe=1024,
                   rb_moe=256)
```

```python
import functools

import numpy as np
import jax
import jax.numpy as jnp
from jax import lax
from jax.experimental import pallas as pl
from jax.experimental.pallas import tpu as pltpu

F32 = jnp.float32
BF16 = jnp.bfloat16
I32 = jnp.int32

D_MODEL = 1024
DEPTH = 4
HEAD_DIM = 64
H_FOX = 4
H_DSA = 4
H_HG = 8
HG_D = 64
N_IDX_HEADS = 8
IDX_DIM = 32
PAGE_SIZE = 128
TOPK_MAX = 256
ROPE_THETA = 500000.0
N_GROUPS = 4
EPG = 4
N_EXPERTS = 16
D_FF = 512
LN_EPS = 1e-5
ALPHA = (2 * DEPTH) ** 0.25
HW = 256
HGW = 512

C_HQ, C_HF, C_HI, C_HG = 0, 512, 1024, 1536
C_FQ, C_FK, C_FV, C_DQ, C_DK, C_DV, C_IQ = (2048 + i * 256 for i in range(7))
C_SM = 3840
N_P = 3968
SM_IK = 32
SM_IW = 64

VMEM_LIMIT = 56 * 1024 * 1024
NEG_INF = float("-inf")


def _cparams(sem):
    return pltpu.CompilerParams(dimension_semantics=sem, vmem_limit_bytes=VMEM_LIMIT)


def _iota(shape, dim):
    return lax.broadcasted_iota(I32, shape, dim)


def _dot(a, b):
    return jnp.dot(a, b, preferred_element_type=F32)


def _dot_nt(a, b):
    return lax.dot_general(a, b, (((1,), (1,)), ((), ())), preferred_element_type=F32)


def _split3(x):
    h = x.astype(BF16)
    r = x - h.astype(F32)
    m = r.astype(BF16)
    l = (r - m.astype(F32)).astype(BF16)
    return h, m, l


def _dot3_l(x, w):
    h, m, l = _split3(x)
    return _dot(h, w) + _dot(m, w) + _dot(l, w)


def _dot3_r(w, x):
    h, m, l = _split3(x)
    return _dot(w, h) + _dot(w, m) + _dot(w, l)


def _dot2_r(w, x):
    h = x.astype(BF16)
    l = (x - h.astype(F32)).astype(BF16)
    return _dot(w, h) + _dot(w, l)


def _layer_norm(y, g, b):
    mu = jnp.mean(y, axis=-1, keepdims=True)
    yc = y - mu
    var = jnp.mean(yc * yc, axis=-1, keepdims=True)
    return yc * lax.rsqrt(var + LN_EPS) * g + b


def _lb_kernel(x_ref, lb_ref, llb_ref, l1m_ref, om_ref):
    x = x_ref[...]
    m = jnp.max(x, axis=0, keepdims=True)
    e = jnp.exp(x - m)
    p = e / jnp.sum(e, axis=0, keepdims=True)
    n = x.shape[0]
    c = p[0:1]
    c0 = c
    for l in range(n):
        if l > 0:
            c = c + p[l:l + 1]
        lb = c - c0
        lb_ref[l:l + 1, :] = lb
        llb_ref[l:l + 1, :] = jnp.log(lb)
        l1m_ref[l:l + 1, :] = jnp.log1p(-lb)
        om_ref[l:l + 1, :] = 1.0 - lb


def lb_params(hg_lb):
    shp = jax.ShapeDtypeStruct(hg_lb.shape, F32)
    return pl.pallas_call(_lb_kernel, out_shape=(shp, shp, shp, shp))(hg_lb.astype(F32))


def _rope(v, t_ref, half):
    n = v.shape[-1]
    return (v * t_ref[0] + pltpu.roll(v, n - half, 1) * t_ref[1]
            + pltpu.roll(v, half, 1) * t_ref[2])


def _proj_kernel(x_ref, w_ref, t64_ref, t32_ref, ts_ref, bf_ref,
                 o_ref, fkt_ref, fvt_ref, dkt_ref, dvt_ref, ikt_ref, lft_ref):
    x = x_ref[...].astype(BF16)

    def seg(a, wd):
        return _dot(x, w_ref[:, a:a + wd])

    o_ref[:, C_HQ:C_HQ + 1024] = seg(C_HQ, 1024)
    o_ref[:, C_HI:C_HI + 1024] = seg(C_HI, 1024)
    o_ref[:, C_FQ:C_FQ + HW] = seg(C_FQ, HW)
    o_ref[:, C_DQ:C_DQ + HW] = _rope(seg(C_DQ, HW), t64_ref, 8)
    o_ref[:, C_IQ:C_IQ + HW] = _rope(seg(C_IQ, HW), t32_ref, 4)
    for c0, t_ref, rot in ((C_FK, fkt_ref, False), (C_FV, fvt_ref, False),
                           (C_DK, dkt_ref, True), (C_DV, dvt_ref, False)):
        v = seg(c0, HW)
        if rot:
            v = _rope(v, t64_ref, 8)
        o_ref[:, c0:c0 + HW] = v
        t_ref[0] = v.T
    sm = _rope(seg(C_SM, 128), ts_ref, 4)
    z = sm + bf_ref[...]
    logsig = jnp.minimum(z, 0.0) - jnp.log1p(jnp.exp(-jnp.abs(z)))
    sm = jnp.where(_iota(sm.shape, 1) < H_FOX, logsig, sm)
    o_ref[:, C_SM:C_SM + 128] = sm
    smt = sm.T
    lft_ref[0] = jnp.where(_iota((8, smt.shape[1]), 0) < H_FOX, smt[0:8], 0.0)
    ikt_ref[0] = smt[SM_IK:SM_IK + IDX_DIM]


def project(x, w_p, t64, t32, ts, bfox_row, tm, nb, seq, tab_tile):
    tps = seq // tm
    tab_map = lambda i: (0, tab_tile(i), 0)
    tspec = lambda w: pl.BlockSpec((1, w, tm), lambda i: (i // tps, 0, i % tps))
    tshape = lambda w: jax.ShapeDtypeStruct((nb, w, seq), F32)
    return pl.pallas_call(
        _proj_kernel,
        grid=(nb * tps,),
        in_specs=[
            pl.BlockSpec((tm, D_MODEL), lambda i: (i, 0)),
            pl.BlockSpec((D_MODEL, N_P), lambda i: (0, 0)),
            pl.BlockSpec((3, tm, HW), tab_map),
            pl.BlockSpec((3, tm, HW), tab_map),
            pl.BlockSpec((3, tm, 128), tab_map),
            pl.BlockSpec((1, 128), lambda i: (0, 0)),
        ],
        out_specs=[pl.BlockSpec((tm, N_P), lambda i: (i, 0)), tspec(HW), tspec(HW), tspec(HW),
                   tspec(HW), tspec(IDX_DIM), tspec(8)],
        out_shape=[jax.ShapeDtypeStruct((nb * seq, N_P), F32), tshape(HW), tshape(HW),
                   tshape(HW), tshape(HW), tshape(IDX_DIM), tshape(8)],
        compiler_params=_cparams(("parallel",)),
    )(x, w_p, t64, t32, ts, bfox_row)


def rope_tables(seq, tm):
    pos = jnp.concatenate([jnp.arange(seq), jnp.full((tm,), seq)]).astype(F32)

    def head_tabs(hd):
        rot = hd // 4
        half = rot // 2
        inv = ROPE_THETA ** (-jnp.arange(half, dtype=F32) * 2.0 / rot)
        ang = pos[:, None] * inv[None, :]
        cos, sin = jnp.cos(ang), jnp.sin(ang)
        n = pos.shape[0]
        one = jnp.ones((n, hd - rot), F32)
        zero = jnp.zeros((n, hd - rot), F32)
        zh = jnp.zeros((n, half), F32)
        c = jnp.concatenate([cos, cos, one], -1)
        sa = jnp.concatenate([-sin, zh, zero], -1)
        sb = jnp.concatenate([zh, sin, zero], -1)
        return c, sa, sb

    t64 = jnp.stack([jnp.tile(a, (1, HW // 64)) for a in head_tabs(64)])
    h32 = head_tabs(32)
    t32 = jnp.stack([jnp.tile(a, (1, HW // 32)) for a in h32])
    n = pos.shape[0]
    fill = [jnp.ones, jnp.zeros, jnp.zeros]
    ts = jnp.stack([jnp.concatenate([f((n, SM_IK), F32), a, f((n, 128 - SM_IK - 32), F32)], -1)
                    for f, a in zip(fill, h32)])
    return t64, t32, ts


def permute_w_in(w_in):
    L = w_in.shape[0]
    z = lambda n: jnp.zeros((L, D_MODEL, n), w_in.dtype)
    ik = w_in[:, :, 1796:1828]
    small = jnp.concatenate([w_in[:, :, 768:772], z(SM_IK - 4), ik, w_in[:, :, 1828:1836],
                             z(128 - SM_IW - 8)], -1)
    w = jnp.concatenate([w_in[:, :, 1836:3884], w_in[:, :, 0:768], w_in[:, :, 772:1540],
                         w_in[:, :, 1540:1796], small], -1)
    return w.astype(BF16)


def _cumsum_kernel(x_ref, tri_ref, o_ref):
    r, n = x_ref.shape
    carry = jnp.zeros((r, 1), F32)
    tri = tri_ref[...]
    for c in range(n // 128):
        blk = x_ref[:, c * 128:(c + 1) * 128]
        cs = _dot3_l(blk, tri) + carry
        o_ref[:, c * 128:(c + 1) * 128] = cs
        carry = carry + jnp.sum(blk, axis=-1, keepdims=True)


def cumsum_rows(x):
    tri = jnp.asarray(np.triu(np.ones((128, 128), np.float32)), BF16)
    return pl.pallas_call(_cumsum_kernel, out_shape=jax.ShapeDtypeStruct(x.shape, F32))(x, tri)


def _fox_prompt_kernel(q_ref, kt_ref, vt_ref, f_ref, o_ref, *, tq, qi):
    nvis = (qi + 1) * tq
    q = q_ref[...] * (HEAD_DIM ** -0.5)
    causal = _iota((tq, nvis), 1) <= _iota((tq, nvis), 0) + qi * tq
    outs = []
    for h in range(H_FOX):
        hs = slice(h * HEAD_DIM, (h + 1) * HEAD_DIM)
        s = _dot(q[:, hs].astype(BF16), kt_ref[0, hs, :].astype(BF16)) - f_ref[0, h:h + 1, :]
        s = jnp.where(causal, s, NEG_INF)
        p = jnp.exp(s - jnp.max(s, axis=-1, keepdims=True))
        l = jnp.sum(p, axis=-1, keepdims=True)
        outs.append(_dot_nt(p.astype(BF16), vt_ref[0, hs, :].astype(BF16)) / l)
    o_ref[0] = jnp.concatenate(outs, axis=-1)


def _stack_blocks(outs, nb, seq):
    return jnp.stack(outs, axis=1).reshape(nb * seq, outs[0].shape[-1])


def fox_prompt(p_all, fkt, fvt, fcum, nb, seq, tq):
    nq = seq // tq
    outs = []
    for qi in range(nq):
        nvis = (qi + 1) * tq
        kspec = lambda w: pl.BlockSpec((1, w, nvis), lambda b: (b, 0, 0))
        outs.append(pl.pallas_call(
            functools.partial(_fox_prompt_kernel, tq=tq, qi=qi),
            grid=(nb,),
            in_specs=[pl.BlockSpec((tq, HW), lambda b, qi=qi: (b * nq + qi, C_FQ // HW)),
                      kspec(HW), kspec(HW), kspec(8)],
            out_specs=pl.BlockSpec((1, tq, HW), lambda b: (b, 0, 0)),
            out_shape=jax.ShapeDtypeStruct((nb, tq, HW), F32),
            compiler_params=_cparams(("parallel",)),
        )(p_all, fkt, fvt, fcum))
    return _stack_blocks(outs, nb, seq)


def _sortable_key(score):
    bits = lax.bitcast_convert_type(score + 0.0, I32)
    return jnp.where(bits < 0, bits ^ jnp.int32(0x7FFFFFFF), bits)


def _select_topk(key_ref, eq_ref, vis, k):
    rows, cols = key_ref.shape
    kf = float(k)

    def body(i, thr):
        cand = thr + jnp.left_shift(jnp.int32(1), 31 - i)
        cnt = jnp.sum(jnp.where(key_ref[...] >= cand, 1.0, 0.0), axis=-1, keepdims=True)
        return jnp.where(cnt >= kf, cand, thr)

    thr = lax.fori_loop(0, 32, body, jnp.full((rows, 1), -2 ** 31, I32))
    key = key_ref[...]
    gt = key > thr
    eqf = jnp.where(key == thr, 1.0, 0.0)
    eq_ref[...] = eqf
    need = kf - jnp.sum(jnp.where(gt, 1.0, 0.0), axis=-1, keepdims=True)
    n_eq = jnp.sum(eqf, axis=-1, keepdims=True)
    nbits = int(cols).bit_length()
    tied = (n_eq > need) & (thr > _sortable_key(jnp.full((1, 1), NEG_INF, F32)))
    any_tied = jnp.max(jnp.where(tied, 1.0, 0.0)) > 0.0

    def search():
        def body2(i, x):
            cand = x + jnp.left_shift(jnp.int32(1), nbits - 1 - i)
            col = _iota((rows, cols), 1)
            c = jnp.sum(jnp.where(col < cand, eq_ref[...], 0.0), axis=-1, keepdims=True)
            return jnp.where(c < need, cand, x)

        return lax.fori_loop(0, nbits, body2, jnp.zeros((rows, 1), I32))

    x = lax.cond(any_tied, search, lambda: jnp.full((rows, 1), cols, I32))
    x = jnp.where(tied, x, cols)
    col = _iota((rows, cols), 1)
    return vis & (gt | ((eq_ref[...] > 0.0) & (col <= x)))


def _dsa_prompt_kernel(dq_ref, iq_ref, sm_ref, dk_ref, dv_ref, ikt_ref, o_ref, key_ref, eq_ref,
                       *, tq, k_sel, qi):
    nvis = (qi + 1) * tq
    iq = iq_ref[...] * (IDX_DIM ** -0.5)
    ikt = jnp.concatenate([ikt_ref[0].astype(BF16)] * N_IDX_HEADS, axis=0)
    sm = sm_ref[...]
    il = _iota(iq.shape, 1) >> 5
    score = jnp.zeros((tq, nvis), F32)
    for h in range(N_IDX_HEADS):
        rel = jnp.maximum(_dot(jnp.where(il == h, iq, 0.0).astype(BF16), ikt), 0.0)
        w = sm[:, SM_IW + h:SM_IW + h + 1] * (N_IDX_HEADS ** -0.5)
        score = score + w * rel
    vis = _iota((tq, nvis), 1) <= _iota((tq, nvis), 0) + qi * tq
    key_ref[...] = _sortable_key(jnp.where(vis, score, NEG_INF))
    sel = _select_topk(key_ref, eq_ref, vis, k_sel)

    dq = dq_ref[...] * (HEAD_DIM ** -0.5)
    dk = dk_ref[0:nvis, :].astype(BF16)
    dv = dv_ref[0:nvis, :].astype(BF16)
    hl = _iota(dq.shape, 1) >> 6
    out = jnp.zeros((tq, HW), F32)
    for h in range(H_DSA):
        s = _dot_nt(jnp.where(hl == h, dq, 0.0).astype(BF16), dk)
        s = jnp.where(sel, s, NEG_INF)
        p = jnp.exp(s - jnp.max(s, axis=-1, keepdims=True))
        l = jnp.sum(p, axis=-1, keepdims=True)
        out = out + jnp.where(hl == h, _dot(p.astype(BF16), dv) / l, 0.0)
    o_ref[0] = out


def dsa_prompt(p_all, ikt, nb, seq, tq):
    nq = seq // tq
    k_sel = min(TOPK_MAX, seq // 4)
    outs = []
    for qi in range(nq):
        nvis = (qi + 1) * tq
        rows = nvis if seq % nvis == 0 else seq
        kspec = lambda c, rows=rows: pl.BlockSpec((rows, HW), lambda b: (b * (seq // rows), c))
        qspec = lambda c, w, qi=qi: pl.BlockSpec((tq, w), lambda b: (b * nq + qi, c))
        outs.append(pl.pallas_call(
            functools.partial(_dsa_prompt_kernel, tq=tq, k_sel=k_sel, qi=qi),
            grid=(nb,),
            in_specs=[qspec(C_DQ // HW, HW), qspec(C_IQ // HW, HW), qspec(C_SM // 128, 128),
                      kspec(C_DK // HW), kspec(C_DV // HW),
                      pl.BlockSpec((1, IDX_DIM, nvis), lambda b: (b, 0, 0))],
            out_specs=pl.BlockSpec((1, tq, HW), lambda b: (b, 0, 0)),
            out_shape=jax.ShapeDtypeStruct((nb, tq, HW), F32),
            scratch_shapes=[pltpu.VMEM((tq, nvis), I32), pltpu.VMEM((tq, nvis), F32)],
            compiler_params=_cparams(("parallel",)),
        )(p_all, p_all, p_all, p_all, p_all, ikt))
    return _stack_blocks(outs, nb, seq)


def hgrn_level_mats(c):
    nl = int(np.log2(c))
    t = np.arange(c)
    mats = [(t[None, :] <= t[:, None]).astype(np.float32)]
    masks = []
    for L in range(nl):
        bit = (t >> L) & 1
        lo = (t >> L) << L
        w = np.zeros((c, c), np.float32)
        for r in range(c):
            if bit[r]:
                w[r, lo[r]:r + 1] = 1.0
            else:
                w[r, r + 1:lo[r] + (1 << L)] = 1.0
        mats.append(w)
        same = (t[:, None] >> (L + 1)) == (t[None, :] >> (L + 1))
        masks.append((bit[:, None] == 1) & (bit[None, :] == 0) & same)
    return np.stack(mats), np.stack(masks).astype(np.float32)


def _hgrn_gates(fl, llb, l1m, om):
    ls = jnp.minimum(fl, 0.0) - jnp.log1p(jnp.exp(-jnp.abs(fl)))
    b = l1m + ls
    logf = jnp.maximum(llb, b) + jnp.log1p(jnp.exp(-jnp.abs(llb - b)))
    kk = om / (1.0 + jnp.exp(fl))
    return logf, kk


def _hgrn_prompt_kernel(q_ref, f_ref, v_ref, g_ref, par_ref, wl_ref, ml_ref, bd_ref, o_ref, st_ref,
                        s_scr, *, c, nl):
    ci = pl.program_id(1)

    @pl.when(ci == 0)
    def _():
        s_scr[...] = jnp.zeros(s_scr.shape, F32)

    q = q_ref[...]
    v = v_ref[...]
    logf, kk = _hgrn_gates(f_ref[...], par_ref[0:1, :], par_ref[1:2, :], par_ref[2:3, :])
    lf3 = jnp.concatenate(_split3(logf), axis=-1)

    e_all = _dot(wl_ref[...], lf3)

    def rowsum(i):
        e = e_all[i * c:(i + 1) * c]
        return e[:, 0:HGW] + e[:, HGW:2 * HGW] + e[:, 2 * HGW:3 * HGW]

    bd = bd_ref[...]
    vb = v.astype(BF16)
    row = _iota((c, HGW), 0)
    hl = _iota((c, HW), 1) >> 6
    a_heads = [jnp.zeros((c, c), F32) for _ in range(H_HG)]
    for L in range(nl):
        e = rowsum(1 + L)
        x = (jnp.where(((row >> L) & 1) == 1, q, kk) * jnp.exp(e)).astype(BF16)
        msk = ml_ref[L] > 0.0
        for h in range(H_HG):
            xh = x[:, (h // 4) * HW:(h // 4 + 1) * HW]
            lhs = jnp.where(hl == (h % 4), xh, jnp.zeros_like(xh))
            a_heads[h] = a_heads[h] + jnp.where(msk, _dot_nt(lhs, xh), 0.0)
    o = _dot((q * kk).astype(BF16), bd) * v
    intra = []
    for half in range(2):
        acc = jnp.zeros((c, HW), F32)
        vh = vb[:, half * HW:(half + 1) * HW]
        for hh in range(4):
            acc = acc + jnp.where(hl == hh, _dot(a_heads[half * 4 + hh].astype(BF16), vh), 0.0)
        intra.append(acc)
    o = o + jnp.concatenate(intra, axis=-1)
    g = rowsum(0)
    st = s_scr[...]
    o = o + _dot_nt((q * jnp.exp(g)).astype(BF16), st.astype(BF16))
    gl = g[c - 1:c, :]
    kd = (kk * jnp.exp(gl - g)).astype(BF16)
    upd = _dot(v.T.astype(BF16), kd)
    s_new = st * jnp.exp(gl) + jnp.where(bd > 0, upd, 0.0)
    s_scr[...] = s_new
    st_ref[0] = s_new

    o2 = o * o
    o2h = o2.astype(BF16)
    o2l = (o2 - o2h.astype(F32)).astype(BF16)
    ms = (_dot(o2h, bd) + _dot(o2l, bd)) * (1.0 / HG_D)
    hg = g_ref[...]
    o_ref[...] = o * lax.rsqrt(ms + LN_EPS) * par_ref[3:4, :] * (hg / (1.0 + jnp.exp(-hg)))


def hgrn_prompt(p_all, par, nb, seq, c):
    nc = seq // c
    nl = int(np.log2(c))
    wl, ml = hgrn_level_mats(c)
    bd = np.kron(np.eye(H_HG, dtype=np.float32), np.ones((HG_D, HG_D), np.float32))
    o, st = pl.pallas_call(
        functools.partial(_hgrn_prompt_kernel, c=c, nl=nl),
        grid=(nb, nc),
        in_specs=[
            pl.BlockSpec((c, HGW), lambda b, i: (b * nc + i, C_HQ // HGW)),
            pl.BlockSpec((c, HGW), lambda b, i: (b * nc + i, C_HF // HGW)),
            pl.BlockSpec((c, HGW), lambda b, i: (b * nc + i, C_HI // HGW)),
            pl.BlockSpec((c, HGW), lambda b, i: (b * nc + i, C_HG // HGW)),
            pl.BlockSpec((8, HGW), lambda b, i: (0, 0)),
            pl.BlockSpec(((nl + 1) * c, c), lambda b, i: (0, 0)),
            pl.BlockSpec((nl, c, c), lambda b, i: (0, 0, 0)),
            pl.BlockSpec((HGW, HGW), lambda b, i: (0, 0)),
        ],
        out_specs=[
            pl.BlockSpec((c, HGW), lambda b, i: (b * nc + i, 0)),
            pl.BlockSpec((1, HGW, HGW), lambda b, i: (b, 0, 0)),
        ],
        out_shape=[jax.ShapeDtypeStruct((nb * seq, HGW), F32),
                   jax.ShapeDtypeStruct((nb, HGW, HGW), F32)],
        scratch_shapes=[pltpu.VMEM((HGW, HGW), F32)],
        compiler_params=_cparams(("parallel", "arbitrary")),
    )(p_all, p_all, p_all, p_all, par, jnp.asarray(wl.reshape(-1, c), BF16), jnp.asarray(ml, F32),
      jnp.asarray(bd, BF16))
    ar = jnp.arange(H_HG)
    st = st.reshape(nb, H_HG, HG_D, H_HG, HG_D)[:, ar, :, ar, :]
    return o, jnp.transpose(st, (1, 0, 3, 2))


def _hgrn_sample_kernel(q_ref, f_ref, v_ref, g_ref, s_ref, pc_ref, n_ref, o_ref, so_ref,
                        f_scr, k_scr):
    logf, kk = _hgrn_gates(f_ref[0], pc_ref[0, 0], pc_ref[1, 0], pc_ref[2, 0])
    f_scr[...] = jnp.exp(logf)
    k_scr[...] = kk
    v = v_ref[0]

    def body(d, o):
        sn = f_scr[pl.ds(d, 1), :] * s_ref[0, 0, d] + k_scr[pl.ds(d, 1), :] * v
        so_ref[0, d] = sn
        return o + q_ref[0, pl.ds(d, 1), :] * sn

    o = lax.fori_loop(0, HG_D, body, jnp.zeros(v.shape, F32), unroll=8)
    ms = jnp.mean(o * o, axis=0, keepdims=True)
    hg = g_ref[0]
    o_ref[0] = o * lax.rsqrt(ms + LN_EPS) * n_ref[0] * (hg / (1.0 + jnp.exp(-hg)))


def hgrn_sample(qt, ft, vt, gt, state_t, layer, par_col, norm_col):
    ns = qt.shape[-1]
    vec = pl.BlockSpec((1, HG_D, ns), lambda h: (h, 0, 0))
    return pl.pallas_call(
        _hgrn_sample_kernel,
        grid=(H_HG,),
        in_specs=[vec, vec, vec, vec,
                  pl.BlockSpec((1, 1, HG_D, HG_D, ns), lambda h: (layer, h, 0, 0, 0)),
                  pl.BlockSpec((3, 1, HG_D, 1), lambda h: (0, h, 0, 0)),
                  pl.BlockSpec((1, HG_D, 1), lambda h: (h, 0, 0))],
        out_specs=[vec, pl.BlockSpec((1, HG_D, HG_D, ns), lambda h: (h, 0, 0, 0))],
        out_shape=[jax.ShapeDtypeStruct((H_HG, HG_D, ns), F32),
                   jax.ShapeDtypeStruct((H_HG, HG_D, HG_D, ns), F32)],
        scratch_shapes=[pltpu.VMEM((HG_D, ns), F32), pltpu.VMEM((HG_D, ns), F32)],
        compiler_params=_cparams(("parallel",)),
    )(qt, ft, vt, gt, state_t, par_col, norm_col)


def _head_rows(q, scale):
    qb = jnp.broadcast_to(q * scale, (8, HW))
    return jnp.where((_iota((8, HW), 1) >> 6) == _iota((8, HW), 0), qb, 0.0)


def _merge_heads(acc):
    return jnp.sum(jnp.where((_iota((8, HW), 1) >> 6) == _iota((8, HW), 0), acc, 0.0),
                   axis=0, keepdims=True)


def _fox_sample_kernel(pt_ref, q_ref, kn_ref, vn_ref, lfn_ref, u_ref, mc_ref, *rest, npages):
    k_refs = rest[0:npages]
    v_refs = rest[npages:2 * npages]
    lf_refs = rest[2 * npages:3 * npages]
    o_ref = rest[3 * npages]
    lf_scr = rest[3 * npages + 1]
    q8 = _head_rows(q_ref[0], HEAD_DIM ** -0.5)
    q8b = q8.astype(BF16)
    lf_scr[...] = jnp.zeros(lf_scr.shape, F32)
    for p in range(npages):
        lf_scr[8 * p:8 * p + H_FOX, :] = lf_refs[p][0, 0]
    lf = lf_scr[...]
    tot = jnp.broadcast_to(jnp.sum(lf, axis=-1, keepdims=True), lf.shape)
    bias = _dot3_l(lf, u_ref[...]) + _dot3_r(mc_ref[...], tot) + lfn_ref[0]
    s_new = jnp.sum(q8 * kn_ref[0], axis=-1, keepdims=True)
    ss = []
    m = s_new
    for p in range(npages):
        s = _dot(q8b, k_refs[p][0, 0].astype(BF16)) + bias[8 * p:8 * p + 8, :]
        ss.append(s)
        m = jnp.maximum(m, jnp.max(s, axis=-1, keepdims=True))
    pn = jnp.exp(s_new - m)
    l = pn
    acc = pn * vn_ref[0]
    for p in range(npages):
        e = jnp.exp(ss[p] - m)
        l = l + jnp.sum(e, axis=-1, keepdims=True)
        acc = acc + _dot_nt(e.astype(BF16), v_refs[p][0, 0].astype(BF16))
    o_ref[0] = _merge_heads(acc / l)


def _page_specs(npages, layer, blk):
    def mk(p):
        return pl.BlockSpec((1, 1) + blk, lambda b, pt: (layer, pt[b, p], 0, 0))
    return [mk(p) for p in range(npages)]


def fox_sample(page_table, q, kn, vn, lfn_col, ck, cv, clf_t, layer):
    ns, npages = page_table.shape
    r = 8 * npages
    u = np.tril(np.ones((PAGE_SIZE, PAGE_SIZE), np.float32), -1)
    pg = np.arange(r) // 8
    hd = np.arange(r) % 8
    mc = ((hd[:, None] == hd[None, :]) & (pg[None, :] > pg[:, None])).astype(np.float32)
    row = pl.BlockSpec((1, 1, HW), lambda b, pt: (b, 0, 0))
    gs = pltpu.PrefetchScalarGridSpec(
        num_scalar_prefetch=1,
        grid=(ns,),
        in_specs=[row, row, row,
                  pl.BlockSpec((1, r, 1), lambda b, pt: (b, 0, 0)),
                  pl.BlockSpec((PAGE_SIZE, PAGE_SIZE), lambda b, pt: (0, 0)),
                  pl.BlockSpec((r, r), lambda b, pt: (0, 0))]
        + _page_specs(npages, layer, (HW, PAGE_SIZE))
        + _page_specs(npages, layer, (HW, PAGE_SIZE))
        + _page_specs(npages, layer, (H_FOX, PAGE_SIZE)),
        out_specs=row,
        scratch_shapes=[pltpu.VMEM((r, PAGE_SIZE), F32)],
    )
    return pl.pallas_call(
        functools.partial(_fox_sample_kernel, npages=npages),
        grid_spec=gs,
        out_shape=jax.ShapeDtypeStruct((ns, 1, HW), F32),
        compiler_params=_cparams(("parallel",)),
    )(page_table, q, kn, vn, lfn_col, jnp.asarray(u, BF16), jnp.asarray(mc, BF16),
      *([ck] * npages), *([cv] * npages), *([clf_t] * npages))


def _dsa_scores_kernel(pt_ref, iq_ref, w_ref, ikn_ref, *rest, npages):
    ik_refs = rest[0:npages]
    o_ref = rest[npages]
    iq = iq_ref[0] * (IDX_DIM ** -0.5)
    w = w_ref[0] * (N_IDX_HEADS ** -0.5)
    iqb = iq.astype(BF16)
    for p in range(npages):
        rel = jnp.maximum(_dot(iqb, ik_refs[p][0, 0].astype(BF16)), 0.0)
        o_ref[0, :, p * PAGE_SIZE:(p + 1) * PAGE_SIZE] = jnp.sum(w * rel, axis=0, keepdims=True)
    rel_n = jnp.maximum(jnp.sum(iq * ikn_ref[0], axis=-1, keepdims=True), 0.0)
    sc_n = jnp.sum(w * rel_n, axis=0, keepdims=True)
    past = npages * PAGE_SIZE
    o_ref[0, :, past:past + PAGE_SIZE] = jnp.where(_iota((1, PAGE_SIZE), 1) == 0, sc_n, NEG_INF)


def dsa_sample_scores(page_table, iq, w_col, ikn, cik, layer):
    ns, npages = page_table.shape
    ncol = (npages + 1) * PAGE_SIZE
    gs = pltpu.PrefetchScalarGridSpec(
        num_scalar_prefetch=1,
        grid=(ns,),
        in_specs=[pl.BlockSpec((1, N_IDX_HEADS, IDX_DIM), lambda b, pt: (b, 0, 0)),
                  pl.BlockSpec((1, N_IDX_HEADS, 1), lambda b, pt: (b, 0, 0)),
                  pl.BlockSpec((1, 1, IDX_DIM), lambda b, pt: (b, 0, 0))]
        + _page_specs(npages, layer, (IDX_DIM, PAGE_SIZE)),
        out_specs=pl.BlockSpec((1, 1, ncol), lambda b, pt: (b, 0, 0)),
    )
    return pl.pallas_call(
        functools.partial(_dsa_scores_kernel, npages=npages),
        grid_spec=gs,
        out_shape=jax.ShapeDtypeStruct((ns, 1, ncol), F32),
        compiler_params=_cparams(("parallel",)),
    )(page_table, iq, w_col, ikn, *([cik] * npages))


def _select_kernel(s_ref, o_ref, key_ref, eq_ref, *, n_vis, k_sel):
    vis = _iota(s_ref.shape, 1) < n_vis
    key_ref[...] = _sortable_key(jnp.where(vis, s_ref[...], NEG_INF))
    sel = _select_topk(key_ref, eq_ref, vis, k_sel)
    o_ref[...] = jnp.where(sel, 1.0, 0.0)


def select_rows(scores, n_vis, k_sel):
    return pl.pallas_call(
        functools.partial(_select_kernel, n_vis=n_vis, k_sel=k_sel),
        out_shape=jax.ShapeDtypeStruct(scores.shape, F32),
        scratch_shapes=[pltpu.VMEM(scores.shape, I32), pltpu.VMEM(scores.shape, F32)],
        compiler_params=pltpu.CompilerParams(vmem_limit_bytes=VMEM_LIMIT),
    )(scores)


def _dsa_sample_kernel(pt_ref, q_ref, kn_ref, vn_ref, mk_ref, *rest, npages):
    k_refs = rest[0:npages]
    v_refs = rest[npages:2 * npages]
    o_ref = rest[2 * npages]
    q8 = _head_rows(q_ref[0], HEAD_DIM ** -0.5)
    q8b = q8.astype(BF16)
    past = npages * PAGE_SIZE
    s_new = jnp.sum(q8 * kn_ref[0], axis=-1, keepdims=True)
    s_new = jnp.where(mk_ref[0, :, past:past + 1] > 0.0, s_new, NEG_INF)
    ss = []
    m = s_new
    for p in range(npages):
        s = _dot(q8b, k_refs[p][0, 0].astype(BF16))
        s = jnp.where(mk_ref[0, :, p * PAGE_SIZE:(p + 1) * PAGE_SIZE] > 0.0, s, NEG_INF)
        ss.append(s)
        m = jnp.maximum(m, jnp.max(s, axis=-1, keepdims=True))
    pn = jnp.exp(s_new - m)
    l = pn
    acc = pn * vn_ref[0]
    for p in range(npages):
        e = jnp.exp(ss[p] - m)
        l = l + jnp.sum(e, axis=-1, keepdims=True)
        acc = acc + _dot_nt(e.astype(BF16), v_refs[p][0, 0].astype(BF16))
    o_ref[0] = _merge_heads(acc / l)


def dsa_sample(page_table, q, kn, vn, mask, ck, cv, layer):
    ns, npages = page_table.shape
    ncol = mask.shape[-1]
    row = pl.BlockSpec((1, 1, HW), lambda b, pt: (b, 0, 0))
    gs = pltpu.PrefetchScalarGridSpec(
        num_scalar_prefetch=1,
        grid=(ns,),
        in_specs=[row, row, row, pl.BlockSpec((1, 1, ncol), lambda b, pt: (b, 0, 0))]
        + _page_specs(npages, layer, (HW, PAGE_SIZE))
        + _page_specs(npages, layer, (HW, PAGE_SIZE)),
        out_specs=row,
    )
    return pl.pallas_call(
        functools.partial(_dsa_sample_kernel, npages=npages),
        grid_spec=gs,
        out_shape=jax.ShapeDtypeStruct((ns, 1, HW), F32),
        compiler_params=_cparams(("parallel",)),
    )(page_table, q, kn, vn, mask, *([ck] * npages), *([cv] * npages))


def _outproj_kernel(x_ref, fo_ref, do_ref, ho_ref, w_ref, g_ref, b_ref, o_ref):
    sub = (_dot(fo_ref[...].astype(BF16), w_ref[0:HW, :])
           + _dot(do_ref[...].astype(BF16), w_ref[HW:2 * HW, :])
           + _dot(ho_ref[...].astype(BF16), w_ref[2 * HW:, :]))
    o_ref[...] = _layer_norm(ALPHA * x_ref[...] + sub, g_ref[...], b_ref[...])


def outproj_norm(x_all, fo, do, ho, w_out, g, b, tm):
    n = x_all.shape[0]
    rowspec = lambda w: pl.BlockSpec((tm, w), lambda i: (i, 0))
    const = lambda shp: pl.BlockSpec(shp, lambda i: (0, 0))
    return pl.pallas_call(
        _outproj_kernel,
        grid=(n // tm,),
        in_specs=[rowspec(D_MODEL), rowspec(HW), rowspec(HW), rowspec(HGW),
                  const((D_MODEL, D_MODEL)), const((1, D_MODEL)), const((1, D_MODEL))],
        out_specs=rowspec(D_MODEL),
        out_shape=jax.ShapeDtypeStruct((n, D_MODEL), F32),
        compiler_params=_cparams(("parallel",)),
    )(x_all, fo, do, ho, w_out, g, b)


def _moe_kernel(x_ref, wrh_ref, wrl_ref, br_ref, tri_ref, w1_ref, w3_ref, w2_ref, g_ref, b_ref,
                o_ref, xs_scr, gate_scr, acc_scr, pos_scr, bnd_ref, *, rb):
    e = pl.program_id(1)
    tm = x_ref.shape[0]
    lane = _iota((tm, 128), 1)

    @pl.when(e == 0)
    def _():
        x = x_ref[...]
        xh = x.astype(BF16)
        xl = (x - xh.astype(F32)).astype(BF16)
        logits = (_dot(xh, wrh_ref[...]) + _dot(xl, wrh_ref[...]) + _dot(xh, wrl_ref[...])
                  + br_ref[...])
        gl = jnp.where(lane < N_GROUPS, logits, NEG_INF)
        gmax = jnp.max(gl, axis=-1, keepdims=True)
        p_top = 1.0 / jnp.sum(jnp.exp(gl - gmax), axis=-1, keepdims=True)
        lanef = lane.astype(F32)
        grp = jnp.min(jnp.where(gl == gmax, lanef, 1024.0), axis=-1, keepdims=True)
        lo = N_GROUPS + EPG * grp
        ev = jnp.where((lanef >= lo) & (lanef < lo + EPG), logits, NEG_INF)
        v1 = jnp.max(ev, axis=-1, keepdims=True)
        i1 = jnp.min(jnp.where(ev == v1, lanef, 1024.0), axis=-1, keepdims=True)
        ev2 = jnp.where(lanef == i1, NEG_INF, ev)
        v2 = jnp.max(ev2, axis=-1, keepdims=True)
        i2 = jnp.min(jnp.where(ev2 == v2, lanef, 1024.0), axis=-1, keepdims=True)
        t = jnp.exp(v2 - v1)
        g1 = p_top / (1.0 + t)
        g2 = p_top * t / (1.0 + t)
        gates = jnp.where(lanef == i1, g1, 0.0) + jnp.where(lanef == i2, g2, 0.0)

        onehot = jnp.where(lanef == grp, 1.0, 0.0)
        earlier = _dot(tri_ref[...], onehot.astype(BF16))
        n = jnp.sum(onehot, axis=0, keepdims=True)
        upper = jnp.where(_iota((128, 128), 0) < _iota((128, 128), 1), 1.0, 0.0).astype(BF16)
        start = _dot3_l(jnp.broadcast_to(n, (8, 128)), upper)[0:1]
        pos = jnp.sum(onehot * (start + earlier), axis=-1, keepdims=True)
        pos_scr[...] = jnp.broadcast_to(pos, (tm, 128))
        pos_row = pos_scr[...].T[0:1, :]
        perm = jnp.where(_iota((tm, tm), 0).astype(F32) == pos_row, 1.0, 0.0).astype(BF16)
        xs_scr[...] = _dot(perm, xh).astype(BF16)
        gate_scr[...] = _dot2_r(perm, gates)
        acc_scr[...] = jnp.zeros(acc_scr.shape, F32)
        lane1 = _iota((1, 128), 1)
        for g in range(N_GROUPS):
            s_g = jnp.sum(jnp.where(lane1 == g, start, 0.0))
            n_g = jnp.sum(jnp.where(lane1 == g, n, 0.0))
            bnd_ref[g] = s_g.astype(I32)
            bnd_ref[N_GROUPS + g] = (s_g + n_g).astype(I32)

    grp_e = e // EPG
    first, last = bnd_ref[grp_e], bnd_ref[N_GROUPS + grp_e]
    for j in range(tm // rb):
        @pl.when((first < (j + 1) * rb) & (last > j * rb))
        def _():
            rows = slice(j * rb, (j + 1) * rb)
            xb = xs_scr[rows, :]
            a = _dot(xb, w1_ref[0])
            h = (a / (1.0 + jnp.exp(-a))) * _dot(xb, w3_ref[0])
            y = _dot(h.astype(BF16), w2_ref[0])
            ge = jnp.sum(jnp.where(_iota((rb, 128), 1) == e + N_GROUPS, gate_scr[rows, :], 0.0),
                         axis=-1, keepdims=True)
            acc_scr[rows, :] += y * ge

    @pl.when(e == N_EXPERTS - 1)
    def _():
        unperm = jnp.where(_iota((tm, tm), 1).astype(F32) == pos_scr[:, 0:1], 1.0, 0.0).astype(BF16)
        y = _dot2_r(unperm, acc_scr[...])
        o_ref[...] = _layer_norm(ALPHA * x_ref[...] + y, g_ref[...], b_ref[...])


def moe_norm(x_all, wr_hi, wr_lo, br, w1, w3, w2, g, b, tm, rb):
    n = x_all.shape[0]
    const = lambda shp: pl.BlockSpec(shp, lambda i, e: (0, 0))
    tri = jnp.asarray(np.tril(np.ones((tm, tm), np.float32), -1), BF16)
    return pl.pallas_call(
        functools.partial(_moe_kernel, rb=rb),
        grid=(n // tm, N_EXPERTS),
        in_specs=[pl.BlockSpec((tm, D_MODEL), lambda i, e: (i, 0)),
                  const((D_MODEL, 128)), const((D_MODEL, 128)), const((1, 128)), const((tm, tm)),
                  pl.BlockSpec((1, D_MODEL, D_FF), lambda i, e: (e, 0, 0)),
                  pl.BlockSpec((1, D_MODEL, D_FF), lambda i, e: (e, 0, 0)),
                  pl.BlockSpec((1, D_FF, D_MODEL), lambda i, e: (e, 0, 0)),
                  const((1, D_MODEL)), const((1, D_MODEL))],
        out_specs=pl.BlockSpec((tm, D_MODEL), lambda i, e: (i, 0)),
        out_shape=jax.ShapeDtypeStruct((n, D_MODEL), F32),
        scratch_shapes=[pltpu.VMEM((tm, D_MODEL), BF16), pltpu.VMEM((tm, 128), F32),
                        pltpu.VMEM((tm, D_MODEL), F32), pltpu.VMEM((tm, 128), F32),
                        pltpu.SMEM((2 * N_GROUPS,), I32)],
        compiler_params=_cparams(("parallel", "arbitrary")),
    )(x_all, wr_hi, wr_lo, br, tri, w1, w3, w2, g, b)


def router_weights(wg, bg, we, be):
    w = jnp.concatenate([wg, we, jnp.zeros((D_MODEL, 128 - N_GROUPS - N_EXPERTS), F32)], -1)
    hi = w.astype(BF16)
    lo = (w - hi.astype(F32)).astype(BF16)
    br = jnp.concatenate([bg, be, jnp.zeros((128 - N_GROUPS - N_EXPERTS,), F32)])[None, :]
    return hi, lo, br


def forward(x_prompt, x_sample, cache_fox_k, cache_fox_v, cache_fox_logf, cache_dsa_k, cache_dsa_v,
            cache_idx_k, state_hgrn, page_table, w_in, b_fox, hg_lb, hg_norm, w_out, ln1_g, ln1_b,
            moe_wg, moe_bg, moe_we, moe_be, moe_w1, moe_w3, moe_w2, ln2_g, ln2_b,
            *, tm, tq_fox, tq_dsa, chunk, tm_moe, rb_moe):
    nb, seq, _ = x_prompt.shape
    ns = x_sample.shape[0]
    depth = w_in.shape[0]
    npr = nb * seq
    npages = page_table.shape[1]
    pool = cache_fox_k.shape[1]
    tps = seq // tm

    pad_rows = lambda a: jnp.concatenate([a, jnp.zeros((tm - ns, a.shape[-1]), F32)], 0)
    xp = x_prompt.reshape(npr, D_MODEL)
    xs = pad_rows(x_sample.reshape(ns, D_MODEL))
    w_p = permute_w_in(w_in)
    t64, t32, ts = rope_tables(seq, tm)
    lb, llb, l1m, om = lb_params(hg_lb)
    pages_t = lambda c: jnp.transpose(c, (0, 1, 3, 4, 2)).reshape(depth, pool, HW, PAGE_SIZE)
    ck_f, cv_f, ck_d, cv_d = (pages_t(c) for c in (cache_fox_k, cache_fox_v, cache_dsa_k,
                                                   cache_dsa_v))
    cik_t = jnp.swapaxes(cache_idx_k, 2, 3)
    clf_t = jnp.swapaxes(cache_fox_logf, 2, 3)
    state_t = jnp.transpose(state_hgrn, (0, 2, 3, 4, 1))
    w_out_b = w_out.astype(BF16)
    w1_b, w3_b, w2_b = moe_w1.astype(BF16), moe_w3.astype(BF16), moe_w2.astype(BF16)
    k_sel_s = min(TOPK_MAX, (npages * PAGE_SIZE + 1) // 4)

    outs_p = [[] for _ in range(7)]
    outs_s = [[] for _ in range(7)]
    for l in range(depth):
        bf_row = jnp.concatenate([b_fox[l], jnp.zeros((128 - H_FOX,), F32)])[None, :]
        pp, fkt, fvt, dkt, dvt, ikt, lft = project(xp, w_p[l], t64, t32, ts, bf_row, tm, nb, seq,
                                                   lambda i: i % tps)
        p_s, fkt_s, fvt_s, dkt_s, dvt_s, ikt_s, lft_s = project(
            xs, w_p[l], t64, t32, ts, bf_row, tm, 1, tm, lambda i: tps)
        ps = p_s[:ns]
        sm_s = ps[:, C_SM:C_SM + 128]

        fcum = cumsum_rows(lft.reshape(nb * 8, seq)).reshape(nb, 8, seq)
        fo_p = fox_prompt(pp, fkt, fvt, fcum, nb, seq, tq_fox)
        do_p = dsa_prompt(pp, ikt, nb, seq, tq_dsa)
        par = jnp.concatenate([llb[l:l + 1], l1m[l:l + 1], om[l:l + 1], hg_norm[l][None, :],
                               jnp.zeros((4, HGW), F32)], 0)
        ho_p, st_p = hgrn_prompt(pp, par, nb, seq, chunk)

        r3 = lambda a: a.reshape(ns, 1, a.shape[-1])
        lfn = jnp.concatenate([sm_s[:, 0:H_FOX], jnp.zeros((ns, 8 - H_FOX), F32)], -1)
        lfn_col = jnp.tile(lfn, (1, npages))[:, :, None]
        fo_s = fox_sample(page_table, r3(ps[:, C_FQ:C_FQ + HW]), r3(ps[:, C_FK:C_FK + HW]),
                          r3(ps[:, C_FV:C_FV + HW]), lfn_col, ck_f, cv_f, clf_t, l)
        iq_s = ps[:, C_IQ:C_IQ + HW].reshape(ns, N_IDX_HEADS, IDX_DIM)
        w_col = sm_s[:, SM_IW:SM_IW + N_IDX_HEADS][:, :, None]
        ik_s = sm_s[:, SM_IK:SM_IK + IDX_DIM]
        sc = dsa_sample_scores(page_table, iq_s, w_col, r3(ik_s), cik_t, l)
        mask = select_rows(sc.reshape(ns, -1), npages * PAGE_SIZE + 1, k_sel_s)
        do_s = dsa_sample(page_table, r3(ps[:, C_DQ:C_DQ + HW]), r3(ps[:, C_DK:C_DK + HW]),
                          r3(ps[:, C_DV:C_DV + HW]), mask[:, None, :], ck_d, cv_d, l)
        tr = lambda c: ps[:, c:c + HGW].T.reshape(H_HG, HG_D, ns)
        par_col = jnp.stack([llb[l], l1m[l], om[l]]).reshape(3, H_HG, HG_D, 1)
        ho_t, st_t = hgrn_sample(tr(C_HQ), tr(C_HF), tr(C_HI), tr(C_HG), state_t, l, par_col,
                                 hg_norm[l].reshape(H_HG, HG_D, 1))
        ho_s = ho_t.reshape(HGW, ns).T
        st_s = jnp.transpose(st_t, (3, 0, 1, 2))

        g1, b1 = ln1_g[l][None, :], ln1_b[l][None, :]
        g2, b2 = ln2_g[l][None, :], ln2_b[l][None, :]
        wr_hi, wr_lo, br = router_weights(moe_wg[l], moe_bg[l], moe_we[l], moe_be[l])
        xp = outproj_norm(xp, fo_p, do_p, ho_p, w_out_b[l], g1, b1, tm)
        xp = moe_norm(xp, wr_hi, wr_lo, br, w1_b[l], w3_b[l], w2_b[l], g2, b2, tm_moe, rb_moe)
        xs = outproj_norm(xs, pad_rows(fo_s.reshape(ns, HW)), pad_rows(do_s.reshape(ns, HW)),
                          pad_rows(ho_s), w_out_b[l], g1, b1, tm)
        xs = moe_norm(xs, wr_hi, wr_lo, br, w1_b[l], w3_b[l], w2_b[l], g2, b2, tm, tm)

        hp = lambda a: jnp.transpose(a.reshape(nb, 4, HEAD_DIM, seq), (0, 3, 1, 2))
        for lst, a in zip(outs_p, (hp(fkt), hp(fvt), jnp.swapaxes(lft[:, 0:H_FOX], 1, 2),
                                   hp(dkt), hp(dvt), jnp.swapaxes(ikt, 1, 2), st_p)):
            lst.append(a)
        hs = lambda a: jnp.transpose(a[0, :, :ns].reshape(4, HEAD_DIM, ns), (2, 0, 1))[:, None]
        for lst, a in zip(outs_s, (hs(fkt_s), hs(fvt_s), lft_s[0, 0:H_FOX, :ns].T[:, None],
                                   hs(dkt_s), hs(dvt_s), ikt_s[0, :, :ns].T[:, None], st_s)):
            lst.append(a)

    y_p = xp.reshape(nb, seq, D_MODEL)
    y_s = xs[:ns].reshape(ns, 1, D_MODEL)
    return (y_p, y_s) + tuple(jnp.stack(a) for a in outs_p) + tuple(jnp.stack(a) for a in outs_s)


def kernel(x_prompt, x_sample, cache_fox_k, cache_fox_v, cache_fox_logf, cache_dsa_k, cache_dsa_v, cache_idx_k, state_hgrn, page_table, w_in, b_fox, hg_lb, hg_norm, w_out, ln1_g, ln1_b, moe_wg, moe_bg, moe_we, moe_be, moe_w1, moe_w3, moe_w2, ln2_g, ln2_b):
    return forward(x_prompt, x_sample, cache_fox_k, cache_fox_v, cache_fox_logf, cache_dsa_k,
                   cache_dsa_v, cache_idx_k, state_hgrn, page_table, w_in, b_fox, hg_lb, hg_norm,
                   w_out, ln1_g, ln1_b, moe_wg, moe_bg, moe_we, moe_be, moe_w1, moe_w3, moe_w2,
                   ln2_g, ln2_b, tm=256, tq_fox=256, tq_dsa=256, chunk=128, tm_moe=1024,
                   rb_moe=256)
```

```python
import functools

import numpy as np
import jax
import jax.numpy as jnp
from jax import lax
from jax.experimental import pallas as pl
from jax.experimental.pallas import tpu as pltpu

F32 = jnp.float32
BF16 = jnp.bfloat16
I32 = jnp.int32

D_MODEL = 1024
DEPTH = 4
HEAD_DIM = 64
H_FOX = 4
H_DSA = 4
H_HG = 8
HG_D = 64
N_IDX_HEADS = 8
IDX_DIM = 32
PAGE_SIZE = 128
TOPK_MAX = 256
ROPE_THETA = 500000.0
N_GROUPS = 4
EPG = 4
N_EXPERTS = 16
D_FF = 512
LN_EPS = 1e-5
ALPHA = (2 * DEPTH) ** 0.25
HW = 256
HGW = 512

C_HQ, C_HF, C_HI, C_HG = 0, 512, 1024, 1536
C_FQ, C_FK, C_FV, C_DQ, C_DK, C_DV, C_IQ = (2048 + i * 256 for i in range(7))
C_SM = 3840
N_P = 3968
SM_IK = 32
SM_IW = 64

VMEM_LIMIT = 56 * 1024 * 1024
NEG_INF = float("-inf")


def _cparams(sem):
    return pltpu.CompilerParams(dimension_semantics=sem, vmem_limit_bytes=VMEM_LIMIT)


def _iota(shape, dim):
    return lax.broadcasted_iota(I32, shape, dim)


def _dot(a, b):
    return jnp.dot(a, b, preferred_element_type=F32)


def _dot_nt(a, b):
    return lax.dot_general(a, b, (((1,), (1,)), ((), ())), preferred_element_type=F32)


def _split3(x):
    h = x.astype(BF16)
    r = x - h.astype(F32)
    m = r.astype(BF16)
    l = (r - m.astype(F32)).astype(BF16)
    return h, m, l


def _dot3_l(x, w):
    h, m, l = _split3(x)
    return _dot(h, w) + _dot(m, w) + _dot(l, w)


def _dot3_r(w, x):
    h, m, l = _split3(x)
    return _dot(w, h) + _dot(w, m) + _dot(w, l)


def _dot2_r(w, x):
    h = x.astype(BF16)
    l = (x - h.astype(F32)).astype(BF16)
    return _dot(w, h) + _dot(w, l)


def _layer_norm(y, g, b):
    mu = jnp.mean(y, axis=-1, keepdims=True)
    yc = y - mu
    var = jnp.mean(yc * yc, axis=-1, keepdims=True)
    return yc * lax.rsqrt(var + LN_EPS) * g + b


def _lb_kernel(x_ref, lb_ref, llb_ref, l1m_ref, om_ref):
    x = x_ref[...]
    m = jnp.max(x, axis=0, keepdims=True)
    e = jnp.exp(x - m)
    p = e / jnp.sum(e, axis=0, keepdims=True)
    n = x.shape[0]
    c = p[0:1]
    c0 = c
    for l in range(n):
        if l > 0:
            c = c + p[l:l + 1]
        lb = c - c0
        lb_ref[l:l + 1, :] = lb
        llb_ref[l:l + 1, :] = jnp.log(lb)
        l1m_ref[l:l + 1, :] = jnp.log1p(-lb)
        om_ref[l:l + 1, :] = 1.0 - lb


def lb_params(hg_lb):
    shp = jax.ShapeDtypeStruct(hg_lb.shape, F32)
    return pl.pallas_call(_lb_kernel, out_shape=(shp, shp, shp, shp))(hg_lb.astype(F32))


def _rope(v, t_ref, half):
    n = v.shape[-1]
    return (v * t_ref[0] + pltpu.roll(v, n - half, 1) * t_ref[1]
            + pltpu.roll(v, half, 1) * t_ref[2])


def _proj_kernel(x_ref, w_ref, t64_ref, t32_ref, ts_ref, bf_ref,
                 o_ref, fkt_ref, fvt_ref, dkt_ref, dvt_ref, ikt_ref, lft_ref):
    x = x_ref[...].astype(BF16)

    def seg(a, wd):
        return _dot(x, w_ref[:, a:a + wd])

    o_ref[:, C_HQ:C_HQ + 1024] = seg(C_HQ, 1024)
    o_ref[:, C_HI:C_HI + 1024] = seg(C_HI, 1024)
    o_ref[:, C_FQ:C_FQ + HW] = seg(C_FQ, HW)
    o_ref[:, C_DQ:C_DQ + HW] = _rope(seg(C_DQ, HW), t64_ref, 8)
    o_ref[:, C_IQ:C_IQ + HW] = _rope(seg(C_IQ, HW), t32_ref, 4)
    for c0, t_ref, rot in ((C_FK, fkt_ref, False), (C_FV, fvt_ref, False),
                           (C_DK, dkt_ref, True), (C_DV, dvt_ref, False)):
        v = seg(c0, HW)
        if rot:
            v = _rope(v, t64_ref, 8)
        o_ref[:, c0:c0 + HW] = v
        t_ref[0] = v.T
    sm = _rope(seg(C_SM, 128), ts_ref, 4)
    z = sm + bf_ref[...]
    logsig = jnp.minimum(z, 0.0) - jnp.log1p(jnp.exp(-jnp.abs(z)))
    sm = jnp.where(_iota(sm.shape, 1) < H_FOX, logsig, sm)
    o_ref[:, C_SM:C_SM + 128] = sm
    smt = sm.T
    lft_ref[0] = jnp.where(_iota((8, smt.shape[1]), 0) < H_FOX, smt[0:8], 0.0)
    ikt_ref[0] = smt[SM_IK:SM_IK + IDX_DIM]


def project(x, w_p, t64, t32, ts, bfox_row, tm, nb, seq, tab_tile):
    tps = seq // tm
    tab_map = lambda i: (0, tab_tile(i), 0)
    tspec = lambda w: pl.BlockSpec((1, w, tm), lambda i: (i // tps, 0, i % tps))
    tshape = lambda w: jax.ShapeDtypeStruct((nb, w, seq), F32)
    return pl.pallas_call(
        _proj_kernel,
        grid=(nb * tps,),
        in_specs=[
            pl.BlockSpec((tm, D_MODEL), lambda i: (i, 0)),
            pl.BlockSpec((D_MODEL, N_P), lambda i: (0, 0)),
            pl.BlockSpec((3, tm, HW), tab_map),
            pl.BlockSpec((3, tm, HW), tab_map),
            pl.BlockSpec((3, tm, 128), tab_map),
            pl.BlockSpec((1, 128), lambda i: (0, 0)),
        ],
        out_specs=[pl.BlockSpec((tm, N_P), lambda i: (i, 0)), tspec(HW), tspec(HW), tspec(HW),
                   tspec(HW), tspec(IDX_DIM), tspec(8)],
        out_shape=[jax.ShapeDtypeStruct((nb * seq, N_P), F32), tshape(HW), tshape(HW),
                   tshape(HW), tshape(HW), tshape(IDX_DIM), tshape(8)],
        compiler_params=_cparams(("parallel",)),
    )(x, w_p, t64, t32, ts, bfox_row)


def rope_tables(seq, tm):
    pos = jnp.concatenate([jnp.arange(seq), jnp.full((tm,), seq)]).astype(F32)

    def head_tabs(hd):
        rot = hd // 4
        half = rot // 2
        inv = ROPE_THETA ** (-jnp.arange(half, dtype=F32) * 2.0 / rot)
        ang = pos[:, None] * inv[None, :]
        cos, sin = jnp.cos(ang), jnp.sin(ang)
        n = pos.shape[0]
        one = jnp.ones((n, hd - rot), F32)
        zero = jnp.zeros((n, hd - rot), F32)
        zh = jnp.zeros((n, half), F32)
        c = jnp.concatenate([cos, cos, one], -1)
        sa = jnp.concatenate([-sin, zh, zero], -1)
        sb = jnp.concatenate([zh, sin, zero], -1)
        return c, sa, sb

    t64 = jnp.stack([jnp.tile(a, (1, HW // 64)) for a in head_tabs(64)])
    h32 = head_tabs(32)
    t32 = jnp.stack([jnp.tile(a, (1, HW // 32)) for a in h32])
    n = pos.shape[0]
    fill = [jnp.ones, jnp.zeros, jnp.zeros]
    ts = jnp.stack([jnp.concatenate([f((n, SM_IK), F32), a, f((n, 128 - SM_IK - 32), F32)], -1)
                    for f, a in zip(fill, h32)])
    return t64, t32, ts


def permute_w_in(w_in):
    L = w_in.shape[0]
    z = lambda n: jnp.zeros((L, D_MODEL, n), w_in.dtype)
    ik = w_in[:, :, 1796:1828]
    small = jnp.concatenate([w_in[:, :, 768:772], z(SM_IK - 4), ik, w_in[:, :, 1828:1836],
                             z(128 - SM_IW - 8)], -1)
    w = jnp.concatenate([w_in[:, :, 1836:3884], w_in[:, :, 0:768], w_in[:, :, 772:1540],
                         w_in[:, :, 1540:1796], small], -1)
    return w.astype(BF16)


def _cumsum_kernel(x_ref, tri_ref, o_ref):
    r, n = x_ref.shape
    carry = jnp.zeros((r, 1), F32)
    tri = tri_ref[...]
    for c in range(n // 128):
        blk = x_ref[:, c * 128:(c + 1) * 128]
        cs = _dot3_l(blk, tri) + carry
        o_ref[:, c * 128:(c + 1) * 128] = cs
        carry = carry + jnp.sum(blk, axis=-1, keepdims=True)


def cumsum_rows(x):
    tri = jnp.asarray(np.triu(np.ones((128, 128), np.float32)), BF16)
    return pl.pallas_call(_cumsum_kernel, out_shape=jax.ShapeDtypeStruct(x.shape, F32))(x, tri)


def _fox_prompt_kernel(q_ref, kt_ref, vt_ref, f_ref, o_ref, *, tq, qi):
    nvis = (qi + 1) * tq
    q = q_ref[...] * (HEAD_DIM ** -0.5)
    causal = _iota((tq, nvis), 1) <= _iota((tq, nvis), 0) + qi * tq
    outs = []
    for h in range(H_FOX):
        hs = slice(h * HEAD_DIM, (h + 1) * HEAD_DIM)
        s = _dot(q[:, hs].astype(BF16), kt_ref[0, hs, :].astype(BF16)) - f_ref[0, h:h + 1, :]
        s = jnp.where(causal, s, NEG_INF)
        p = jnp.exp(s - jnp.max(s, axis=-1, keepdims=True))
        l = jnp.sum(p, axis=-1, keepdims=True)
        outs.append(_dot_nt(p.astype(BF16), vt_ref[0, hs, :].astype(BF16)) / l)
    o_ref[0] = jnp.concatenate(outs, axis=-1)


def _stack_blocks(outs, nb, seq):
    return jnp.stack(outs, axis=1).reshape(nb * seq, outs[0].shape[-1])


def fox_prompt(p_all, fkt, fvt, fcum, nb, seq, tq):
    nq = seq // tq
    outs = []
    for qi in range(nq):
        nvis = (qi + 1) * tq
        kspec = lambda w: pl.BlockSpec((1, w, nvis), lambda b: (b, 0, 0))
        outs.append(pl.pallas_call(
            functools.partial(_fox_prompt_kernel, tq=tq, qi=qi),
            grid=(nb,),
            in_specs=[pl.BlockSpec((tq, HW), lambda b, qi=qi: (b * nq + qi, C_FQ // HW)),
                      kspec(HW), kspec(HW), kspec(8)],
            out_specs=pl.BlockSpec((1, tq, HW), lambda b: (b, 0, 0)),
            out_shape=jax.ShapeDtypeStruct((nb, tq, HW), F32),
            compiler_params=_cparams(("parallel",)),
        )(p_all, fkt, fvt, fcum))
    return _stack_blocks(outs, nb, seq)


KEY_NEG_INF = -2 ** 31 + 0x7FFFFF


def _code_to_float(code):
    bits = jnp.where(code < 0, code ^ jnp.int32(0x7FFFFFFF), code)
    return lax.bitcast_convert_type(bits, F32)


def _select_topk(sc_ref, eq_ref, vis, k):
    rows, cols = sc_ref.shape
    kf = float(k)
    tiny = float(np.finfo(np.float32).tiny)
    sc0 = sc_ref[...]
    sc0 = jnp.where(jnp.abs(sc0) < tiny, jnp.where(lax.bitcast_convert_type(sc0, I32) < 0, -tiny, 0.0),
                    sc0)
    sc_ref[...] = sc0

    def body(i, code):
        cand = code + jnp.left_shift(jnp.int32(1), 31 - i)
        cnt = jnp.sum(jnp.where(sc_ref[...] >= _code_to_float(cand), 1.0, 0.0), axis=-1,
                      keepdims=True)
        return jnp.where(cnt >= kf, cand, code)

    code = lax.fori_loop(0, 32, body, jnp.full((rows, 1), -2 ** 31, I32))
    thr = jnp.where(code < KEY_NEG_INF, NEG_INF, _code_to_float(code))
    thr = jnp.where(jnp.abs(thr) < tiny, 0.0, thr)
    sc = sc_ref[...]
    gt = sc > thr
    eqf = jnp.where(sc == thr, 1.0, 0.0)
    eq_ref[...] = eqf
    need = kf - jnp.sum(jnp.where(gt, 1.0, 0.0), axis=-1, keepdims=True)
    n_eq = jnp.sum(eqf, axis=-1, keepdims=True)
    nbits = int(cols).bit_length()
    tied = (n_eq > need) & (thr > NEG_INF)
    any_tied = jnp.max(jnp.where(tied, 1.0, 0.0)) > 0.0

    def search():
        def body2(i, x):
            cand = x + jnp.left_shift(jnp.int32(1), nbits - 1 - i)
            col = _iota((rows, cols), 1)
            c = jnp.sum(jnp.where(col < cand, eq_ref[...], 0.0), axis=-1, keepdims=True)
            return jnp.where(c < need, cand, x)

        return lax.fori_loop(0, nbits, body2, jnp.zeros((rows, 1), I32))

    x = lax.cond(any_tied, search, lambda: jnp.full((rows, 1), cols, I32))
    x = jnp.where(tied, x, cols)
    col = _iota((rows, cols), 1)
    return vis & (gt | ((eq_ref[...] > 0.0) & (col <= x)))


def _dsa_prompt_kernel(dq_ref, iq_ref, sm_ref, dk_ref, dv_ref, ikt_ref, o_ref, key_ref, eq_ref,
                       *, tq, k_sel, qi):
    nvis = (qi + 1) * tq
    iq = iq_ref[...] * (IDX_DIM ** -0.5)
    ikt = jnp.concatenate([ikt_ref[0].astype(BF16)] * N_IDX_HEADS, axis=0)
    sm = sm_ref[...]
    il = _iota(iq.shape, 1) >> 5
    score = jnp.zeros((tq, nvis), F32)
    for h in range(N_IDX_HEADS):
        rel = jnp.maximum(_dot(jnp.where(il == h, iq, 0.0).astype(BF16), ikt), 0.0)
        w = sm[:, SM_IW + h:SM_IW + h + 1] * (N_IDX_HEADS ** -0.5)
        score = score + w * rel
    vis = _iota((tq, nvis), 1) <= _iota((tq, nvis), 0) + qi * tq
    key_ref[...] = jnp.where(vis, score, NEG_INF)
    sel = _select_topk(key_ref, eq_ref, vis, k_sel)

    dq = dq_ref[...] * (HEAD_DIM ** -0.5)
    dk = dk_ref[0:nvis, :].astype(BF16)
    dv = dv_ref[0:nvis, :].astype(BF16)
    hl = _iota(dq.shape, 1) >> 6
    out = jnp.zeros((tq, HW), F32)
    for h in range(H_DSA):
        s = _dot_nt(jnp.where(hl == h, dq, 0.0).astype(BF16), dk)
        s = jnp.where(sel, s, NEG_INF)
        p = jnp.exp(s - jnp.max(s, axis=-1, keepdims=True))
        l = jnp.sum(p, axis=-1, keepdims=True)
        out = out + jnp.where(hl == h, _dot(p.astype(BF16), dv) / l, 0.0)
    o_ref[0] = out


def dsa_prompt(p_all, ikt, nb, seq, tq):
    nq = seq // tq
    k_sel = min(TOPK_MAX, seq // 4)
    outs = []
    for qi in range(nq):
        nvis = (qi + 1) * tq
        rows = nvis if seq % nvis == 0 else seq
        kspec = lambda c, rows=rows: pl.BlockSpec((rows, HW), lambda b: (b * (seq // rows), c))
        qspec = lambda c, w, qi=qi: pl.BlockSpec((tq, w), lambda b: (b * nq + qi, c))
        outs.append(pl.pallas_call(
            functools.partial(_dsa_prompt_kernel, tq=tq, k_sel=k_sel, qi=qi),
            grid=(nb,),
            in_specs=[qspec(C_DQ // HW, HW), qspec(C_IQ // HW, HW), qspec(C_SM // 128, 128),
                      kspec(C_DK // HW), kspec(C_DV // HW),
                      pl.BlockSpec((1, IDX_DIM, nvis), lambda b: (b, 0, 0))],
            out_specs=pl.BlockSpec((1, tq, HW), lambda b: (b, 0, 0)),
            out_shape=jax.ShapeDtypeStruct((nb, tq, HW), F32),
            scratch_shapes=[pltpu.VMEM((tq, nvis), F32), pltpu.VMEM((tq, nvis), F32)],
            compiler_params=_cparams(("parallel",)),
        )(p_all, p_all, p_all, p_all, p_all, ikt))
    return _stack_blocks(outs, nb, seq)


def hgrn_level_mats(c):
    nl = int(np.log2(c))
    t = np.arange(c)
    mats = [(t[None, :] <= t[:, None]).astype(np.float32)]
    masks = []
    for L in range(nl):
        bit = (t >> L) & 1
        lo = (t >> L) << L
        w = np.zeros((c, c), np.float32)
        for r in range(c):
            if bit[r]:
                w[r, lo[r]:r + 1] = 1.0
            else:
                w[r, r + 1:lo[r] + (1 << L)] = 1.0
        mats.append(w)
        same = (t[:, None] >> (L + 1)) == (t[None, :] >> (L + 1))
        masks.append((bit[:, None] == 1) & (bit[None, :] == 0) & same)
    return np.stack(mats), np.stack(masks).astype(np.float32)


def _hgrn_gates(fl, llb, l1m, om):
    ls = jnp.minimum(fl, 0.0) - jnp.log1p(jnp.exp(-jnp.abs(fl)))
    b = l1m + ls
    logf = jnp.maximum(llb, b) + jnp.log1p(jnp.exp(-jnp.abs(llb - b)))
    kk = om / (1.0 + jnp.exp(fl))
    return logf, kk


def _hgrn_prompt_kernel(q_ref, f_ref, v_ref, g_ref, par_ref, wl_ref, ml_ref, bd_ref, o_ref, st_ref,
                        s_scr, *, c, nl):
    ci = pl.program_id(1)

    @pl.when(ci == 0)
    def _():
        s_scr[...] = jnp.zeros(s_scr.shape, F32)

    q = q_ref[...]
    v = v_ref[...]
    logf, kk = _hgrn_gates(f_ref[...], par_ref[0:1, :], par_ref[1:2, :], par_ref[2:3, :])
    lf3 = jnp.concatenate(_split3(logf), axis=-1)

    e_all = _dot(wl_ref[...], lf3)

    def rowsum(i):
        e = e_all[i * c:(i + 1) * c]
        return e[:, 0:HGW] + e[:, HGW:2 * HGW] + e[:, 2 * HGW:3 * HGW]

    bd = bd_ref[...]
    vb = v.astype(BF16)
    row = _iota((c, HGW), 0)
    hl = _iota((c, HW), 1) >> 6
    a_heads = [jnp.zeros((c, c), F32) for _ in range(H_HG)]
    for L in range(nl):
        e = rowsum(1 + L)
        x = (jnp.where(((row >> L) & 1) == 1, q, kk) * jnp.exp(e)).astype(BF16)
        msk = ml_ref[L] > 0.0
        for h in range(H_HG):
            xh = x[:, (h // 4) * HW:(h // 4 + 1) * HW]
            lhs = jnp.where(hl == (h % 4), xh, jnp.zeros_like(xh))
            a_heads[h] = a_heads[h] + jnp.where(msk, _dot_nt(lhs, xh), 0.0)
    o = _dot((q * kk).astype(BF16), bd) * v
    intra = []
    for half in range(2):
        acc = jnp.zeros((c, HW), F32)
        vh = vb[:, half * HW:(half + 1) * HW]
        for hh in range(4):
            acc = acc + jnp.where(hl == hh, _dot(a_heads[half * 4 + hh].astype(BF16), vh), 0.0)
        intra.append(acc)
    o = o + jnp.concatenate(intra, axis=-1)
    g = rowsum(0)
    st = s_scr[...]
    o = o + _dot_nt((q * jnp.exp(g)).astype(BF16), st.astype(BF16))
    gl = g[c - 1:c, :]
    kd = (kk * jnp.exp(gl - g)).astype(BF16)
    upd = _dot(v.T.astype(BF16), kd)
    s_new = st * jnp.exp(gl) + jnp.where(bd > 0, upd, 0.0)
    s_scr[...] = s_new
    st_ref[0] = s_new

    o2 = o * o
    o2h = o2.astype(BF16)
    o2l = (o2 - o2h.astype(F32)).astype(BF16)
    ms = (_dot(o2h, bd) + _dot(o2l, bd)) * (1.0 / HG_D)
    hg = g_ref[...]
    o_ref[...] = o * lax.rsqrt(ms + LN_EPS) * par_ref[3:4, :] * (hg / (1.0 + jnp.exp(-hg)))


def hgrn_prompt(p_all, par, nb, seq, c):
    nc = seq // c
    nl = int(np.log2(c))
    wl, ml = hgrn_level_mats(c)
    bd = np.kron(np.eye(H_HG, dtype=np.float32), np.ones((HG_D, HG_D), np.float32))
    o, st = pl.pallas_call(
        functools.partial(_hgrn_prompt_kernel, c=c, nl=nl),
        grid=(nb, nc),
        in_specs=[
            pl.BlockSpec((c, HGW), lambda b, i: (b * nc + i, C_HQ // HGW)),
            pl.BlockSpec((c, HGW), lambda b, i: (b * nc + i, C_HF // HGW)),
            pl.BlockSpec((c, HGW), lambda b, i: (b * nc + i, C_HI // HGW)),
            pl.BlockSpec((c, HGW), lambda b, i: (b * nc + i, C_HG // HGW)),
            pl.BlockSpec((8, HGW), lambda b, i: (0, 0)),
            pl.BlockSpec(((nl + 1) * c, c), lambda b, i: (0, 0)),
            pl.BlockSpec((nl, c, c), lambda b, i: (0, 0, 0)),
            pl.BlockSpec((HGW, HGW), lambda b, i: (0, 0)),
        ],
        out_specs=[
            pl.BlockSpec((c, HGW), lambda b, i: (b * nc + i, 0)),
            pl.BlockSpec((1, HGW, HGW), lambda b, i: (b, 0, 0)),
        ],
        out_shape=[jax.ShapeDtypeStruct((nb * seq, HGW), F32),
                   jax.ShapeDtypeStruct((nb, HGW, HGW), F32)],
        scratch_shapes=[pltpu.VMEM((HGW, HGW), F32)],
        compiler_params=_cparams(("parallel", "arbitrary")),
    )(p_all, p_all, p_all, p_all, par, jnp.asarray(wl.reshape(-1, c), BF16), jnp.asarray(ml, F32),
      jnp.asarray(bd, BF16))
    ar = jnp.arange(H_HG)
    st = st.reshape(nb, H_HG, HG_D, H_HG, HG_D)[:, ar, :, ar, :]
    return o, jnp.transpose(st, (1, 0, 3, 2))


def _hgrn_sample_kernel(q_ref, f_ref, v_ref, g_ref, s_ref, pc_ref, n_ref, o_ref, so_ref,
                        f_scr, k_scr):
    logf, kk = _hgrn_gates(f_ref[0], pc_ref[0, 0], pc_ref[1, 0], pc_ref[2, 0])
    f_scr[...] = jnp.exp(logf)
    k_scr[...] = kk
    v = v_ref[0]

    def body(d, o):
        sn = f_scr[pl.ds(d, 1), :] * s_ref[0, 0, d] + k_scr[pl.ds(d, 1), :] * v
        so_ref[0, d] = sn
        return o + q_ref[0, pl.ds(d, 1), :] * sn

    o = lax.fori_loop(0, HG_D, body, jnp.zeros(v.shape, F32), unroll=8)
    ms = jnp.mean(o * o, axis=0, keepdims=True)
    hg = g_ref[0]
    o_ref[0] = o * lax.rsqrt(ms + LN_EPS) * n_ref[0] * (hg / (1.0 + jnp.exp(-hg)))


def hgrn_sample(qt, ft, vt, gt, state_t, layer, par_col, norm_col):
    ns = qt.shape[-1]
    vec = pl.BlockSpec((1, HG_D, ns), lambda h: (h, 0, 0))
    return pl.pallas_call(
        _hgrn_sample_kernel,
        grid=(H_HG,),
        in_specs=[vec, vec, vec, vec,
                  pl.BlockSpec((1, 1, HG_D, HG_D, ns), lambda h: (layer, h, 0, 0, 0)),
                  pl.BlockSpec((3, 1, HG_D, 1), lambda h: (0, h, 0, 0)),
                  pl.BlockSpec((1, HG_D, 1), lambda h: (h, 0, 0))],
        out_specs=[vec, pl.BlockSpec((1, HG_D, HG_D, ns), lambda h: (h, 0, 0, 0))],
        out_shape=[jax.ShapeDtypeStruct((H_HG, HG_D, ns), F32),
                   jax.ShapeDtypeStruct((H_HG, HG_D, HG_D, ns), F32)],
        scratch_shapes=[pltpu.VMEM((HG_D, ns), F32), pltpu.VMEM((HG_D, ns), F32)],
        compiler_params=_cparams(("parallel",)),
    )(qt, ft, vt, gt, state_t, par_col, norm_col)


def _head_rows(q, scale):
    qb = jnp.broadcast_to(q * scale, (8, HW))
    return jnp.where((_iota((8, HW), 1) >> 6) == _iota((8, HW), 0), qb, 0.0)


def _merge_heads(acc):
    return jnp.sum(jnp.where((_iota((8, HW), 1) >> 6) == _iota((8, HW), 0), acc, 0.0),
                   axis=0, keepdims=True)


def _fox_sample_kernel(pt_ref, q_ref, kn_ref, vn_ref, lfn_ref, u_ref, mc_ref, *rest, npages):
    k_refs = rest[0:npages]
    v_refs = rest[npages:2 * npages]
    lf_refs = rest[2 * npages:3 * npages]
    o_ref = rest[3 * npages]
    lf_scr = rest[3 * npages + 1]
    q8 = _head_rows(q_ref[0], HEAD_DIM ** -0.5)
    q8b = q8.astype(BF16)
    lf_scr[...] = jnp.zeros(lf_scr.shape, F32)
    for p in range(npages):
        lf_scr[8 * p:8 * p + H_FOX, :] = lf_refs[p][0, 0]
    lf = lf_scr[...]
    tot = jnp.broadcast_to(jnp.sum(lf, axis=-1, keepdims=True), lf.shape)
    bias = _dot3_l(lf, u_ref[...]) + _dot3_r(mc_ref[...], tot) + lfn_ref[0]
    s_new = jnp.sum(q8 * kn_ref[0], axis=-1, keepdims=True)
    ss = []
    m = s_new
    for p in range(npages):
        s = _dot(q8b, k_refs[p][0, 0].astype(BF16)) + bias[8 * p:8 * p + 8, :]
        ss.append(s)
        m = jnp.maximum(m, jnp.max(s, axis=-1, keepdims=True))
    pn = jnp.exp(s_new - m)
    l = pn
    acc = pn * vn_ref[0]
    for p in range(npages):
        e = jnp.exp(ss[p] - m)
        l = l + jnp.sum(e, axis=-1, keepdims=True)
        acc = acc + _dot_nt(e.astype(BF16), v_refs[p][0, 0].astype(BF16))
    o_ref[0] = _merge_heads(acc / l)


def _page_specs(npages, layer, blk):
    def mk(p):
        return pl.BlockSpec((1, 1) + blk, lambda b, pt: (layer, pt[b, p], 0, 0))
    return [mk(p) for p in range(npages)]


def _fox_and_scores_kernel(pt_ref, q_ref, kn_ref, vn_ref, lfn_ref, u_ref, mc_ref, iq_ref, w_ref,
                           ikn_ref, *rest, npages):
    pages, (o_ref, sc_ref, lf_scr) = rest[:4 * npages], rest[4 * npages:]
    _fox_sample_kernel(pt_ref, q_ref, kn_ref, vn_ref, lfn_ref, u_ref, mc_ref,
                       *pages[:3 * npages], o_ref, lf_scr, npages=npages)
    _dsa_scores_kernel(pt_ref, iq_ref, w_ref, ikn_ref, *pages[3 * npages:], sc_ref, npages=npages)


def fox_and_scores_sample(page_table, q, kn, vn, lfn_col, ck, cv, clf_t, iq, w_col, ikn, cik,
                          layer):
    ns, npages = page_table.shape
    r = 8 * npages
    ncol = (npages + 1) * PAGE_SIZE
    u = np.tril(np.ones((PAGE_SIZE, PAGE_SIZE), np.float32), -1)
    pg = np.arange(r) // 8
    hd = np.arange(r) % 8
    mc = ((hd[:, None] == hd[None, :]) & (pg[None, :] > pg[:, None])).astype(np.float32)
    row = pl.BlockSpec((1, 1, HW), lambda b, pt: (b, 0, 0))
    gs = pltpu.PrefetchScalarGridSpec(
        num_scalar_prefetch=1,
        grid=(ns,),
        in_specs=[row, row, row,
                  pl.BlockSpec((1, r, 1), lambda b, pt: (b, 0, 0)),
                  pl.BlockSpec((PAGE_SIZE, PAGE_SIZE), lambda b, pt: (0, 0)),
                  pl.BlockSpec((r, r), lambda b, pt: (0, 0)),
                  pl.BlockSpec((1, N_IDX_HEADS, IDX_DIM), lambda b, pt: (b, 0, 0)),
                  pl.BlockSpec((1, N_IDX_HEADS, 1), lambda b, pt: (b, 0, 0)),
                  pl.BlockSpec((1, 1, IDX_DIM), lambda b, pt: (b, 0, 0))]
        + _page_specs(npages, layer, (HW, PAGE_SIZE))
        + _page_specs(npages, layer, (HW, PAGE_SIZE))
        + _page_specs(npages, layer, (H_FOX, PAGE_SIZE))
        + _page_specs(npages, layer, (IDX_DIM, PAGE_SIZE)),
        out_specs=[row, pl.BlockSpec((1, 1, ncol), lambda b, pt: (b, 0, 0))],
        scratch_shapes=[pltpu.VMEM((r, PAGE_SIZE), F32)],
    )
    return pl.pallas_call(
        functools.partial(_fox_and_scores_kernel, npages=npages),
        grid_spec=gs,
        out_shape=[jax.ShapeDtypeStruct((ns, 1, HW), F32),
                   jax.ShapeDtypeStruct((ns, 1, ncol), F32)],
        compiler_params=_cparams(("parallel",)),
    )(page_table, q, kn, vn, lfn_col, jnp.asarray(u, BF16), jnp.asarray(mc, BF16), iq, w_col, ikn,
      *([ck] * npages), *([cv] * npages), *([clf_t] * npages), *([cik] * npages))


def _dsa_scores_kernel(pt_ref, iq_ref, w_ref, ikn_ref, *rest, npages):
    ik_refs = rest[0:npages]
    o_ref = rest[npages]
    iq = iq_ref[0] * (IDX_DIM ** -0.5)
    w = w_ref[0] * (N_IDX_HEADS ** -0.5)
    iqb = iq.astype(BF16)
    for p in range(npages):
        rel = jnp.maximum(_dot(iqb, ik_refs[p][0, 0].astype(BF16)), 0.0)
        o_ref[0, :, p * PAGE_SIZE:(p + 1) * PAGE_SIZE] = jnp.sum(w * rel, axis=0, keepdims=True)
    rel_n = jnp.maximum(jnp.sum(iq * ikn_ref[0], axis=-1, keepdims=True), 0.0)
    sc_n = jnp.sum(w * rel_n, axis=0, keepdims=True)
    past = npages * PAGE_SIZE
    o_ref[0, :, past:past + PAGE_SIZE] = jnp.where(_iota((1, PAGE_SIZE), 1) == 0, sc_n, NEG_INF)


def _select_kernel(s_ref, o_ref, key_ref, eq_ref, *, n_vis, k_sel):
    vis = _iota(s_ref.shape, 1) < n_vis
    key_ref[...] = jnp.where(vis, s_ref[...], NEG_INF)
    sel = _select_topk(key_ref, eq_ref, vis, k_sel)
    o_ref[...] = jnp.where(sel, 1.0, 0.0)


def select_rows(scores, n_vis, k_sel):
    return pl.pallas_call(
        functools.partial(_select_kernel, n_vis=n_vis, k_sel=k_sel),
        out_shape=jax.ShapeDtypeStruct(scores.shape, F32),
        scratch_shapes=[pltpu.VMEM(scores.shape, F32), pltpu.VMEM(scores.shape, F32)],
        compiler_params=pltpu.CompilerParams(vmem_limit_bytes=VMEM_LIMIT),
    )(scores)


def _dsa_sample_kernel(pt_ref, q_ref, kn_ref, vn_ref, mk_ref, *rest, npages):
    k_refs = rest[0:npages]
    v_refs = rest[npages:2 * npages]
    o_ref = rest[2 * npages]
    q8 = _head_rows(q_ref[0], HEAD_DIM ** -0.5)
    q8b = q8.astype(BF16)
    past = npages * PAGE_SIZE
    s_new = jnp.sum(q8 * kn_ref[0], axis=-1, keepdims=True)
    s_new = jnp.where(mk_ref[0, :, past:past + 1] > 0.0, s_new, NEG_INF)
    ss = []
    m = s_new
    for p in range(npages):
        s = _dot(q8b, k_refs[p][0, 0].astype(BF16))
        s = jnp.where(mk_ref[0, :, p * PAGE_SIZE:(p + 1) * PAGE_SIZE] > 0.0, s, NEG_INF)
        ss.append(s)
        m = jnp.maximum(m, jnp.max(s, axis=-1, keepdims=True))
    pn = jnp.exp(s_new - m)
    l = pn
    acc = pn * vn_ref[0]
    for p in range(npages):
        e = jnp.exp(ss[p] - m)
        l = l + jnp.sum(e, axis=-1, keepdims=True)
        acc = acc + _dot_nt(e.astype(BF16), v_refs[p][0, 0].astype(BF16))
    o_ref[0] = _merge_heads(acc / l)


def dsa_sample(page_table, q, kn, vn, mask, ck, cv, layer):
    ns, npages = page_table.shape
    ncol = mask.shape[-1]
    row = pl.BlockSpec((1, 1, HW), lambda b, pt: (b, 0, 0))
    gs = pltpu.PrefetchScalarGridSpec(
        num_scalar_prefetch=1,
        grid=(ns,),
        in_specs=[row, row, row, pl.BlockSpec((1, 1, ncol), lambda b, pt: (b, 0, 0))]
        + _page_specs(npages, layer, (HW, PAGE_SIZE))
        + _page_specs(npages, layer, (HW, PAGE_SIZE)),
        out_specs=row,
    )
    return pl.pallas_call(
        functools.partial(_dsa_sample_kernel, npages=npages),
        grid_spec=gs,
        out_shape=jax.ShapeDtypeStruct((ns, 1, HW), F32),
        compiler_params=_cparams(("parallel",)),
    )(page_table, q, kn, vn, mask, *([ck] * npages), *([cv] * npages))


def _outproj_kernel(x_ref, fo_ref, do_ref, ho_ref, w_ref, g_ref, b_ref, o_ref):
    sub = (_dot(fo_ref[...].astype(BF16), w_ref[0:HW, :])
           + _dot(do_ref[...].astype(BF16), w_ref[HW:2 * HW, :])
           + _dot(ho_ref[...].astype(BF16), w_ref[2 * HW:, :]))
    o_ref[...] = _layer_norm(ALPHA * x_ref[...] + sub, g_ref[...], b_ref[...])


def outproj_norm(x_all, fo, do, ho, w_out, g, b, tm):
    n = x_all.shape[0]
    rowspec = lambda w: pl.BlockSpec((tm, w), lambda i: (i, 0))
    const = lambda shp: pl.BlockSpec(shp, lambda i: (0, 0))
    return pl.pallas_call(
        _outproj_kernel,
        grid=(n // tm,),
        in_specs=[rowspec(D_MODEL), rowspec(HW), rowspec(HW), rowspec(HGW),
                  const((D_MODEL, D_MODEL)), const((1, D_MODEL)), const((1, D_MODEL))],
        out_specs=rowspec(D_MODEL),
        out_shape=jax.ShapeDtypeStruct((n, D_MODEL), F32),
        compiler_params=_cparams(("parallel",)),
    )(x_all, fo, do, ho, w_out, g, b)


def _moe_kernel(x_ref, wrh_ref, wrl_ref, br_ref, tri_ref, w1_ref, w3_ref, w2_ref, g_ref, b_ref,
                o_ref, xs_scr, gate_scr, acc_scr, pos_scr, bnd_ref, *, rb):
    e = pl.program_id(1)
    tm = x_ref.shape[0]
    lane = _iota((tm, 128), 1)

    @pl.when(e == 0)
    def _():
        x = x_ref[...]
        xh = x.astype(BF16)
        xl = (x - xh.astype(F32)).astype(BF16)
        logits = (_dot(xh, wrh_ref[...]) + _dot(xl, wrh_ref[...]) + _dot(xh, wrl_ref[...])
                  + br_ref[...])
        gl = jnp.where(lane < N_GROUPS, logits, NEG_INF)
        gmax = jnp.max(gl, axis=-1, keepdims=True)
        p_top = 1.0 / jnp.sum(jnp.exp(gl - gmax), axis=-1, keepdims=True)
        lanef = lane.astype(F32)
        grp = jnp.min(jnp.where(gl == gmax, lanef, 1024.0), axis=-1, keepdims=True)
        lo = N_GROUPS + EPG * grp
        ev = jnp.where((lanef >= lo) & (lanef < lo + EPG), logits, NEG_INF)
        v1 = jnp.max(ev, axis=-1, keepdims=True)
        i1 = jnp.min(jnp.where(ev == v1, lanef, 1024.0), axis=-1, keepdims=True)
        ev2 = jnp.where(lanef == i1, NEG_INF, ev)
        v2 = jnp.max(ev2, axis=-1, keepdims=True)
        i2 = jnp.min(jnp.where(ev2 == v2, lanef, 1024.0), axis=-1, keepdims=True)
        t = jnp.exp(v2 - v1)
        g1 = p_top / (1.0 + t)
        g2 = p_top * t / (1.0 + t)
        gates = jnp.where(lanef == i1, g1, 0.0) + jnp.where(lanef == i2, g2, 0.0)

        onehot = jnp.where(lanef == grp, 1.0, 0.0)
        earlier = _dot(tri_ref[...], onehot.astype(BF16))
        n = jnp.sum(onehot, axis=0, keepdims=True)
        upper = jnp.where(_iota((128, 128), 0) < _iota((128, 128), 1), 1.0, 0.0).astype(BF16)
        start = _dot3_l(jnp.broadcast_to(n, (8, 128)), upper)[0:1]
        pos = jnp.sum(onehot * (start + earlier), axis=-1, keepdims=True)
        pos_scr[...] = jnp.broadcast_to(pos, (tm, 128))
        pos_row = pos_scr[...].T[0:1, :]
        perm = jnp.where(_iota((tm, tm), 0).astype(F32) == pos_row, 1.0, 0.0).astype(BF16)
        xs_scr[...] = _dot(perm, xh).astype(BF16)
        gate_scr[...] = _dot2_r(perm, gates)
        acc_scr[...] = jnp.zeros(acc_scr.shape, F32)
        lane1 = _iota((1, 128), 1)
        for g in range(N_GROUPS):
            s_g = jnp.sum(jnp.where(lane1 == g, start, 0.0))
            n_g = jnp.sum(jnp.where(lane1 == g, n, 0.0))
            bnd_ref[g] = s_g.astype(I32)
            bnd_ref[N_GROUPS + g] = (s_g + n_g).astype(I32)

    grp_e = e // EPG
    first, last = bnd_ref[grp_e], bnd_ref[N_GROUPS + grp_e]
    for j in range(tm // rb):
        @pl.when((first < (j + 1) * rb) & (last > j * rb))
        def _():
            rows = slice(j * rb, (j + 1) * rb)
            xb = xs_scr[rows, :]
            a = _dot(xb, w1_ref[0])
            h = (a / (1.0 + jnp.exp(-a))) * _dot(xb, w3_ref[0])
            y = _dot(h.astype(BF16), w2_ref[0])
            ge = jnp.sum(jnp.where(_iota((rb, 128), 1) == e + N_GROUPS, gate_scr[rows, :], 0.0),
                         axis=-1, keepdims=True)
            acc_scr[rows, :] += y * ge

    @pl.when(e == N_EXPERTS - 1)
    def _():
        unperm = jnp.where(_iota((tm, tm), 1).astype(F32) == pos_scr[:, 0:1], 1.0, 0.0).astype(BF16)
        y = _dot2_r(unperm, acc_scr[...])
        o_ref[...] = _layer_norm(ALPHA * x_ref[...] + y, g_ref[...], b_ref[...])


def moe_norm(x_all, wr_hi, wr_lo, br, w1, w3, w2, g, b, tm, rb):
    n = x_all.shape[0]
    const = lambda shp: pl.BlockSpec(shp, lambda i, e: (0, 0))
    tri = jnp.asarray(np.tril(np.ones((tm, tm), np.float32), -1), BF16)
    return pl.pallas_call(
        functools.partial(_moe_kernel, rb=rb),
        grid=(n // tm, N_EXPERTS),
        in_specs=[pl.BlockSpec((tm, D_MODEL), lambda i, e: (i, 0)),
                  const((D_MODEL, 128)), const((D_MODEL, 128)), const((1, 128)), const((tm, tm)),
                  pl.BlockSpec((1, D_MODEL, D_FF), lambda i, e: (e, 0, 0)),
                  pl.BlockSpec((1, D_MODEL, D_FF), lambda i, e: (e, 0, 0)),
                  pl.BlockSpec((1, D_FF, D_MODEL), lambda i, e: (e, 0, 0)),
                  const((1, D_MODEL)), const((1, D_MODEL))],
        out_specs=pl.BlockSpec((tm, D_MODEL), lambda i, e: (i, 0)),
        out_shape=jax.ShapeDtypeStruct((n, D_MODEL), F32),
        scratch_shapes=[pltpu.VMEM((tm, D_MODEL), BF16), pltpu.VMEM((tm, 128), F32),
                        pltpu.VMEM((tm, D_MODEL), F32), pltpu.VMEM((tm, 128), F32),
                        pltpu.SMEM((2 * N_GROUPS,), I32)],
        compiler_params=_cparams(("parallel", "arbitrary")),
    )(x_all, wr_hi, wr_lo, br, tri, w1, w3, w2, g, b)


def router_weights(wg, bg, we, be):
    w = jnp.concatenate([wg, we, jnp.zeros((D_MODEL, 128 - N_GROUPS - N_EXPERTS), F32)], -1)
    hi = w.astype(BF16)
    lo = (w - hi.astype(F32)).astype(BF16)
    br = jnp.concatenate([bg, be, jnp.zeros((128 - N_GROUPS - N_EXPERTS,), F32)])[None, :]
    return hi, lo, br


def forward(x_prompt, x_sample, cache_fox_k, cache_fox_v, cache_fox_logf, cache_dsa_k, cache_dsa_v,
            cache_idx_k, state_hgrn, page_table, w_in, b_fox, hg_lb, hg_norm, w_out, ln1_g, ln1_b,
            moe_wg, moe_bg, moe_we, moe_be, moe_w1, moe_w3, moe_w2, ln2_g, ln2_b,
            *, tm, tq_fox, tq_dsa, chunk, tm_moe, rb_moe):
    nb, seq, _ = x_prompt.shape
    ns = x_sample.shape[0]
    depth = w_in.shape[0]
    npr = nb * seq
    npages = page_table.shape[1]
    pool = cache_fox_k.shape[1]
    tps = seq // tm

    pad_rows = lambda a: jnp.concatenate([a, jnp.zeros((tm - ns, a.shape[-1]), F32)], 0)
    xp = x_prompt.reshape(npr, D_MODEL)
    xs = pad_rows(x_sample.reshape(ns, D_MODEL))
    w_p = permute_w_in(w_in)
    t64, t32, ts = rope_tables(seq, tm)
    lb, llb, l1m, om = lb_params(hg_lb)
    pages_t = lambda c: jnp.transpose(c, (0, 1, 3, 4, 2)).reshape(depth, pool, HW, PAGE_SIZE)
    ck_f, cv_f, ck_d, cv_d = (pages_t(c) for c in (cache_fox_k, cache_fox_v, cache_dsa_k,
                                                   cache_dsa_v))
    cik_t = jnp.swapaxes(cache_idx_k, 2, 3)
    clf_t = jnp.swapaxes(cache_fox_logf, 2, 3)
    state_t = jnp.transpose(state_hgrn, (0, 2, 3, 4, 1))
    w_out_b = w_out.astype(BF16)
    w1_b, w3_b, w2_b = moe_w1.astype(BF16), moe_w3.astype(BF16), moe_w2.astype(BF16)
    k_sel_s = min(TOPK_MAX, (npages * PAGE_SIZE + 1) // 4)

    outs_p = [[] for _ in range(7)]
    outs_s = [[] for _ in range(7)]
    for l in range(depth):
        bf_row = jnp.concatenate([b_fox[l], jnp.zeros((128 - H_FOX,), F32)])[None, :]
        pp, fkt, fvt, dkt, dvt, ikt, lft = project(xp, w_p[l], t64, t32, ts, bf_row, tm, nb, seq,
                                                   lambda i: i % tps)
        p_s, fkt_s, fvt_s, dkt_s, dvt_s, ikt_s, lft_s = project(
            xs, w_p[l], t64, t32, ts, bf_row, tm, 1, tm, lambda i: tps)
        ps = p_s[:ns]
        sm_s = ps[:, C_SM:C_SM + 128]

        fcum = cumsum_rows(lft.reshape(nb * 8, seq)).reshape(nb, 8, seq)
        fo_p = fox_prompt(pp, fkt, fvt, fcum, nb, seq, tq_fox)
        do_p = dsa_prompt(pp, ikt, nb, seq, tq_dsa)
        par = jnp.concatenate([llb[l:l + 1], l1m[l:l + 1], om[l:l + 1], hg_norm[l][None, :],
                               jnp.zeros((4, HGW), F32)], 0)
        ho_p, st_p = hgrn_prompt(pp, par, nb, seq, chunk)

        r3 = lambda a: a.reshape(ns, 1, a.shape[-1])
        lfn = jnp.concatenate([sm_s[:, 0:H_FOX], jnp.zeros((ns, 8 - H_FOX), F32)], -1)
        lfn_col = jnp.tile(lfn, (1, npages))[:, :, None]
        iq_s = ps[:, C_IQ:C_IQ + HW].reshape(ns, N_IDX_HEADS, IDX_DIM)
        w_col = sm_s[:, SM_IW:SM_IW + N_IDX_HEADS][:, :, None]
        ik_s = sm_s[:, SM_IK:SM_IK + IDX_DIM]
        fo_s, sc = fox_and_scores_sample(
            page_table, r3(ps[:, C_FQ:C_FQ + HW]), r3(ps[:, C_FK:C_FK + HW]),
            r3(ps[:, C_FV:C_FV + HW]), lfn_col, ck_f, cv_f, clf_t, iq_s, w_col, r3(ik_s), cik_t, l)
        mask = select_rows(sc.reshape(ns, -1), npages * PAGE_SIZE + 1, k_sel_s)
        do_s = dsa_sample(page_table, r3(ps[:, C_DQ:C_DQ + HW]), r3(ps[:, C_DK:C_DK + HW]),
                          r3(ps[:, C_DV:C_DV + HW]), mask[:, None, :], ck_d, cv_d, l)
        tr = lambda c: ps[:, c:c + HGW].T.reshape(H_HG, HG_D, ns)
        par_col = jnp.stack([llb[l], l1m[l], om[l]]).reshape(3, H_HG, HG_D, 1)
        ho_t, st_t = hgrn_sample(tr(C_HQ), tr(C_HF), tr(C_HI), tr(C_HG), state_t, l, par_col,
                                 hg_norm[l].reshape(H_HG, HG_D, 1))
        ho_s = ho_t.reshape(HGW, ns).T
        st_s = jnp.transpose(st_t, (3, 0, 1, 2))

        g1, b1 = ln1_g[l][None, :], ln1_b[l][None, :]
        g2, b2 = ln2_g[l][None, :], ln2_b[l][None, :]
        wr_hi, wr_lo, br = router_weights(moe_wg[l], moe_bg[l], moe_we[l], moe_be[l])
        xp = outproj_norm(xp, fo_p, do_p, ho_p, w_out_b[l], g1, b1, tm)
        xp = moe_norm(xp, wr_hi, wr_lo, br, w1_b[l], w3_b[l], w2_b[l], g2, b2, tm_moe, rb_moe)
        xs = outproj_norm(xs, pad_rows(fo_s.reshape(ns, HW)), pad_rows(do_s.reshape(ns, HW)),
                          pad_rows(ho_s), w_out_b[l], g1, b1, tm)
        xs = moe_norm(xs, wr_hi, wr_lo, br, w1_b[l], w3_b[l], w2_b[l], g2, b2, tm, tm)

        hp = lambda a: jnp.transpose(a.reshape(nb, 4, HEAD_DIM, seq), (0, 3, 1, 2))
        for lst, a in zip(outs_p, (hp(fkt), hp(fvt), jnp.swapaxes(lft[:, 0:H_FOX], 1, 2),
                                   hp(dkt), hp(dvt), jnp.swapaxes(ikt, 1, 2), st_p)):
            lst.append(a)
        hs = lambda a: jnp.transpose(a[0, :, :ns].reshape(4, HEAD_DIM, ns), (2, 0, 1))[:, None]
        for lst, a in zip(outs_s, (hs(fkt_s), hs(fvt_s), lft_s[0, 0:H_FOX, :ns].T[:, None],
                                   hs(dkt_s), hs(dvt_s), ikt_s[0, :, :ns].T[:, None], st_s)):
            lst.append(a)

    y_p = xp.reshape(nb, seq, D_MODEL)
    y_s = xs[:ns].reshape(ns, 1, D_MODEL)
    return (y_p, y_s) + tuple(jnp.stack(a) for a in outs_p) + tuple(jnp.stack(a) for a in outs_s)


def kernel(x_prompt, x_sample, cache_fox_k, cache_fox_v, cache_fox_logf, cache_dsa_k, cache_dsa_v, cache_idx_k, state_hgrn, page_table, w_in, b_fox, hg_lb, hg_norm, w_out, ln1_g, ln1_b, moe_wg, moe_bg, moe_we, moe_be, moe_w1, moe_w3, moe_w2, ln2_g, ln2_b):
    return forward(x_prompt, x_sample, cache_fox_k, cache_fox_v, cache_fox_logf, cache_dsa_k,
                   cache_dsa_v, cache_idx_k, state_hgrn, page_table, w_in, b_fox, hg_lb, hg_norm,
                   w_out, ln1_g, ln1_b, moe_wg, moe_bg, moe_we, moe_be, moe_w1, moe_w3, moe_w2,
                   ln2_g, ln2_b, tm=256, tq_fox=256, tq_dsa=256, chunk=128, tm_moe=1024,
                   rb_moe=256)
```
